```python
import jax, jax.numpy as jnp
from jax import lax
import numpy as np

D_MODEL = 1024
BATCH = 4
SEQ = 8192
DEPTH = 2

D_MIX = D_MODEL
N_MIXERS = 4
W_GROUP = D_MIX // N_MIXERS
HEAD_DIM = 64
N_HEADS = W_GROUP // HEAD_DIM
CONV_A_WIDTH = 31
POOL_WINDOWS = (2, 4, 8, 16)
POOL_GROUP = W_GROUP // len(POOL_WINDOWS)
CHUNK = 128
SHORT_CONV_WIDTH = 3
FFN_CONV_WIDTH = 3
D_FF = ((8 * D_MODEL // 3 + 127) // 128) * 128
N_MOD = 6
IN_A = 2 * W_GROUP
IN_B = W_GROUP
IN_C = 2 * W_GROUP
IN_D = 3 * W_GROUP
D_IN = IN_A + IN_B + IN_C + IN_D
EPS = 1e-6

kernel_name = "hybrid_conv_pool_sgu_shortconv_block"


def rms_norm(x, g):
    xf = x.astype(jnp.float32)
    y = xf * lax.rsqrt(jnp.mean(xf * xf, axis=-1, keepdims=True) + EPS)
    return (y * g.astype(jnp.float32)).astype(x.dtype)


def group_norm(x, n_groups, g, b):
    shp = x.shape
    xf = x.astype(jnp.float32).reshape(shp[:-1] + (n_groups, shp[-1] // n_groups))
    mu = jnp.mean(xf, axis=-1, keepdims=True)
    var = jnp.mean(jnp.square(xf - mu), axis=-1, keepdims=True)
    y = ((xf - mu) * lax.rsqrt(var + EPS)).reshape(shp)
    return (y * g.astype(jnp.float32) + b.astype(jnp.float32)).astype(x.dtype)


def causal_dwconv(x, w):
    k, ch = w.shape
    return lax.conv_general_dilated(
        x, w[:, None, :].astype(x.dtype), window_strides=(1,), padding=[(k - 1, 0)],
        dimension_numbers=("NWC", "WIO", "NWC"), feature_group_count=ch)


def conformer_conv(pa, conv_w, conv_b, gn_g, gn_b):
    a, g = jnp.split(pa, 2, axis=-1)
    h = a * jax.nn.sigmoid(g)
    h = causal_dwconv(h, conv_w) + conv_b
    h = group_norm(h, N_HEADS, gn_g, gn_b)
    return jax.nn.silu(h)


def pool_mixer(pb, pool_w, pool_scale):
    bn, s, ch = pb.shape
    xf = pb.astype(jnp.float32)
    csp = jnp.concatenate([jnp.zeros((bn, 1, ch), jnp.float32), jnp.cumsum(xf, axis=1)], axis=1)
    upper = csp[:, 1:]
    pos = jnp.arange(1, s + 1, dtype=jnp.float32)[None, :, None]
    outs = []
    for gi, w in enumerate(POOL_WINDOWS):
        sl = slice(gi * POOL_GROUP, (gi + 1) * POOL_GROUP)
        lower = jnp.concatenate([jnp.zeros((bn, w - 1, POOL_GROUP), jnp.float32), csp[:, :s + 1 - w, sl]], axis=1)
        mean = (upper[..., sl] - lower) / jnp.minimum(pos, w)
        outs.append(mean - xf[..., sl])
    y = jnp.stack(outs, axis=2).astype(pb.dtype)
    y = jnp.einsum("bsgc,gcd->bsgd", y, pool_w).reshape(bn, s, ch)
    return y * pool_scale


def spatial_gating(pc, ln_g, ln_b, w_s, b_s):
    u, v = jnp.split(pc, 2, axis=-1)
    v = group_norm(v, 1, ln_g, ln_b)
    bn, s, ch = v.shape
    v = v.reshape(bn, s // CHUNK, CHUNK, N_HEADS, HEAD_DIM)
    mask = jnp.tril(jnp.ones((CHUNK, CHUNK), dtype=bool))
    w = jnp.where(mask[None], w_s, jnp.zeros_like(w_s))
    sv = jnp.einsum("hts,bnshc->bnthc", w, v) + jnp.transpose(b_s)[None, None, :, :, None]
    return u * sv.reshape(bn, s, ch)


def short_conv(pd, conv_w):
    bg, cg, h = jnp.split(pd, 3, axis=-1)
    return bg * causal_dwconv(cg * h, conv_w)


def setup_inputs(seed: int = 0) -> dict:
    key = jax.random.key(seed)
    ks = jax.random.split(key, 24)
    f32 = jnp.float32
    L = DEPTH

    def nrm(k, shape, scale):
        return scale * jax.random.normal(k, shape, f32)

    return {
        "x": nrm(ks[0], (BATCH, SEQ, D_MODEL), 1.0),
        "c": nrm(ks[1], (BATCH, D_MODEL), 1.0),
        "norm1_g": 1.0 + nrm(ks[2], (L, D_MODEL), 0.05),
        "ada_w": nrm(ks[3], (L, D_MODEL, N_MOD * D_MODEL), 0.5 * D_MODEL ** -0.5),
        "ada_b": nrm(ks[4], (L, N_MOD * D_MODEL), 0.02),
        "w_in": nrm(ks[5], (L, D_MODEL, D_IN), D_MODEL ** -0.5),
        "conv_a_w": nrm(ks[6], (L, CONV_A_WIDTH, W_GROUP), CONV_A_WIDTH ** -0.5),
        "conv_a_b": nrm(ks[7], (L, W_GROUP), 0.02),
        "gn_a_g": 1.0 + nrm(ks[8], (L, W_GROUP), 0.05),
        "gn_a_b": nrm(ks[9], (L, W_GROUP), 0.02),
        "pool_w": nrm(ks[10], (L, len(POOL_WINDOWS), POOL_GROUP, POOL_GROUP), POOL_GROUP ** -0.5),
        "pool_scale": 1.0 + nrm(ks[11], (L, W_GROUP), 0.1),
        "sgu_ln_g": 1.0 + nrm(ks[12], (L, W_GROUP), 0.05),
        "sgu_ln_b": nrm(ks[13], (L, W_GROUP), 0.02),
        "sgu_w": nrm(ks[14], (L, N_HEADS, CHUNK, CHUNK), CHUNK ** -0.5),
        "sgu_b": 1.0 + nrm(ks[15], (L, N_HEADS, CHUNK), 0.1),
        "conv_d_w": nrm(ks[16], (L, SHORT_CONV_WIDTH, W_GROUP), SHORT_CONV_WIDTH ** -0.5),
        "w_out": nrm(ks[17], (L, D_MIX, D_MODEL), D_MIX ** -0.5),
        "norm2_g": 1.0 + nrm(ks[18], (L, D_MODEL), 0.05),
        "ffn_w_gate": nrm(ks[19], (L, D_MODEL, D_FF), D_MODEL ** -0.5),
        "ffn_w_up": nrm(ks[20], (L, D_MODEL, D_FF), D_MODEL ** -0.5),
        "ffn_conv_w": nrm(ks[21], (L, FFN_CONV_WIDTH, D_FF), FFN_CONV_WIDTH ** -0.5),
        "ffn_w_down": nrm(ks[22], (L, D_FF, D_MODEL), D_FF ** -0.5),
        "final_g": 1.0 + nrm(ks[23], (D_MODEL,), 0.05),
    }


def reference(x, c, norm1_g, ada_w, ada_b, w_in, conv_a_w, conv_a_b, gn_a_g, gn_a_b,
              pool_w, pool_scale, sgu_ln_g, sgu_ln_b, sgu_w, sgu_b, conv_d_w, w_out,
              norm2_g, ffn_w_gate, ffn_w_up, ffn_conv_w, ffn_w_down, final_g):
    c_act = jax.nn.silu(c)
    for l in range(DEPTH):
        mod = (c_act @ ada_w[l] + ada_b[l])[:, None, :]
        sh1, sc1, g1, sh2, sc2, g2 = jnp.split(mod, N_MOD, axis=-1)

        h = rms_norm(x, norm1_g[l]) * (1 + sc1) + sh1
        p = h @ w_in[l]
        pa, pb, pc, pd = jnp.split(p, [IN_A, IN_A + IN_B, IN_A + IN_B + IN_C], axis=-1)
        ya = conformer_conv(pa, conv_a_w[l], conv_a_b[l], gn_a_g[l], gn_a_b[l])
        yb = pool_mixer(pb, pool_w[l], pool_scale[l])
        yc = spatial_gating(pc, sgu_ln_g[l], sgu_ln_b[l], sgu_w[l], sgu_b[l])
        yd = short_conv(pd, conv_d_w[l])
        y = jnp.concatenate([ya, yb, yc, yd], axis=-1) @ w_out[l]
        x = x + g1 * y

        h = rms_norm(x, norm2_g[l]) * (1 + sc2) + sh2
        a = jax.nn.silu(causal_dwconv(h @ ffn_w_gate[l], ffn_conv_w[l]))
        f = (a * (h @ ffn_w_up[l])) @ ffn_w_down[l]
        x = x + g2 * f
    return rms_norm(x, final_g)
```

```python
import functools

import jax
import jax.numpy as jnp
from jax import lax
from jax.experimental import pallas as pl
from jax.experimental.pallas import tpu as pltpu

EPS = 1e-6
N_MOD = 6
W_GROUP = 256
HEAD_DIM = 64
N_HEADS = W_GROUP // HEAD_DIM
CHUNK = 128
CONV_A_WIDTH = 31
POOL_WINDOWS = (2, 4, 8, 16)
SHORT_CONV_WIDTH = 3
FFN_CONV_WIDTH = 3

SUBLANES = 8
A_HALO = 32
B_HALO = 16
D_HALO = 8
F_HALO = 8
CONV_ROW_BLOCK = 64

SEQ_TILE = 512
MOD_COL_BLOCK = 2048
VMEM_LIMIT_BYTES = 56 * 1024 * 1024

F32 = jnp.float32
BF16 = jnp.bfloat16


def _dot(a, b):
    return jnp.dot(a, b, preferred_element_type=F32)


def _sigmoid(v):
    return 1.0 / (1.0 + jnp.exp(-v))


def _split_dot(v, m_bf16):
    hi = v.astype(BF16)
    lo = (v - hi.astype(F32)).astype(BF16)
    return _dot(hi, m_bf16) + _dot(lo, m_bf16)


def _modulated_rms_norm(x, gain, scale, shift):
    ms = jnp.mean(x * x, axis=-1, keepdims=True)
    return (x * lax.rsqrt(ms + EPS)) * (gain * (1.0 + scale)) + shift


def _mod_kernel(c_ref, w_ref, b_ref, o_ref):
    c = c_ref[...]
    ca = c * _sigmoid(c)
    o_ref[0] = jnp.dot(ca, w_ref[0], preferred_element_type=F32,
                       precision=lax.Precision.HIGHEST) + b_ref[0]


def _mod_call(c_pad, ada_w, ada_b):
    n_layers, d_model, n_out = ada_w.shape
    rows = c_pad.shape[0]
    nb = MOD_COL_BLOCK
    return pl.pallas_call(
        _mod_kernel,
        grid=(n_layers, n_out // nb),
        in_specs=[
            pl.BlockSpec((rows, d_model), lambda l, j: (0, 0)),
            pl.BlockSpec((1, d_model, nb), lambda l, j: (l, 0, j)),
            pl.BlockSpec((1, 1, nb), lambda l, j: (l, 0, j)),
        ],
        out_specs=pl.BlockSpec((1, rows, nb), lambda l, j: (l, 0, j)),
        out_shape=jax.ShapeDtypeStruct((n_layers, rows, n_out), F32),
        compiler_params=pltpu.CompilerParams(
            dimension_semantics=("arbitrary", "arbitrary"),
            vmem_limit_bytes=VMEM_LIMIT_BYTES),
        name="mod",
    )(c_pad, ada_w, ada_b.reshape(n_layers, 1, n_out))


def _conformer_conv(glu, abuf, caw_ref, cab_ref, gng_ref, gnb_ref, gmat, ts):
    abuf[pl.ds(A_HALO, ts), :] = glu
    rb = CONV_ROW_BLOCK
    first_off = A_HALO - (CONV_A_WIDTH - 1)
    blocks = []
    for r0 in range(0, ts, rb):
        acc = None
        for b in range(SUBLANES):
            taps = [(o - first_off, o - b) for o in range(first_off, A_HALO + 1)
                    if o % SUBLANES == b]
            span = max(off for _, off in taps)
            win = abuf[pl.ds(r0 + b, rb + span), :]
            part = None
            for k, off in taps:
                term = caw_ref[k:k + 1, :] * win[off:off + rb]
                part = term if part is None else part + term
            acc = part if acc is None else acc + part
        blocks.append(acc)
    conv = jnp.concatenate(blocks, axis=0) + cab_ref[...]
    abuf[pl.ds(0, A_HALO), :] = abuf[pl.ds(ts, A_HALO), :]

    inv = 1.0 / HEAD_DIM
    mu = _split_dot(conv, gmat) * inv
    d = conv - mu
    var = _split_dot(d * d, gmat) * inv
    y = d * lax.rsqrt(var + EPS) * gng_ref[...] + gnb_ref[...]
    return y * _sigmoid(y)


def _pool_mixer(pb, bbuf, poolw_ref, pools_ref, ts, is_first):
    bbuf[pl.ds(B_HALO, ts), :] = pb
    half = W_GROUP // 2
    lane = lax.broadcasted_iota(jnp.int32, (1, half), 1)
    low = lane < (half // 2)

    def shifted(j, c0):
        return bbuf[pl.ds(B_HALO - j, ts), c0:c0 + half]

    x0 = pb[:, 0:half]
    s2 = x0 + shifted(1, 0)
    s4 = s2 + shifted(2, 0) + shifted(3, 0)
    sum_a = jnp.where(low, s2, s4)
    s8 = pb[:, half:W_GROUP]
    for j in range(1, 8):
        s8 = s8 + shifted(j, half)
    s16 = s8
    for j in range(8, 16):
        s16 = s16 + shifted(j, half)
    sum_b = jnp.where(low, s8, s16)
    sums = jnp.concatenate([sum_a, sum_b], axis=-1)

    bbuf[pl.ds(0, B_HALO), :] = bbuf[pl.ds(ts, B_HALO), :]

    lane_full = lax.broadcasted_iota(jnp.int32, (1, W_GROUP), 1)
    win = jnp.where(lane_full < 64, 2.0,
                    jnp.where(lane_full < 128, 4.0, jnp.where(lane_full < 192, 8.0, 16.0)))
    pos = lax.broadcasted_iota(jnp.int32, (B_HALO, W_GROUP), 0).astype(F32) + 1.0
    count = jnp.where(is_first, jnp.minimum(pos, win), win)
    mean = jnp.concatenate([sums[0:B_HALO] / count, sums[B_HALO:] * (1.0 / win)], axis=0)
    y = mean - pb
    return _dot(y.astype(BF16), poolw_ref[...]) * pools_ref[...]


def _spatial_gating(u, v, lng_ref, lnb_ref, sguw_ref, sgub_ref, ts):
    mu = jnp.mean(v, axis=-1, keepdims=True)
    d = v - mu
    var = jnp.mean(d * d, axis=-1, keepdims=True)
    vn = (d * lax.rsqrt(var + EPS) * lng_ref[...] + lnb_ref[...]).astype(BF16)

    row = lax.broadcasted_iota(jnp.int32, (CHUNK, N_HEADS * CHUNK), 0)
    col = lax.broadcasted_iota(jnp.int32, (CHUNK, N_HEADS * CHUNK), 1)
    wcat = jnp.where((col & (CHUNK - 1)) <= row, sguw_ref[...], 0.0).astype(BF16)

    head_of_lane = lax.broadcasted_iota(jnp.int32, (CHUNK, W_GROUP), 1) // HEAD_DIM
    zero = jnp.zeros((CHUNK, W_GROUP), BF16)
    outs = []
    for c0 in range(0, ts, CHUNK):
        vc = vn[c0:c0 + CHUNK]
        vstack = jnp.concatenate(
            [jnp.where(head_of_lane == h, vc, zero) for h in range(N_HEADS)], axis=0)
        outs.append(_dot(wcat, vstack) + sgub_ref[...])
    return u * jnp.concatenate(outs, axis=0)


def _short_conv(bg, cgh, dbuf, cdw_ref, ts):
    dbuf[pl.ds(D_HALO, ts), :] = cgh
    conv = cdw_ref[SHORT_CONV_WIDTH - 1:SHORT_CONV_WIDTH, :] * cgh
    for k in range(SHORT_CONV_WIDTH - 1):
        back = SHORT_CONV_WIDTH - 1 - k
        conv = conv + cdw_ref[k:k + 1, :] * dbuf[pl.ds(D_HALO - back, ts), :]
    dbuf[pl.ds(0, D_HALO), :] = dbuf[pl.ds(ts, D_HALO), :]
    return bg * conv


def _mix_kernel(x_ref, mod_ref, n1g_ref, w_in_ref, caw_ref, cab_ref, gng_ref, gnb_ref, gmat_ref,
                poolw_ref, pools_ref, lng_ref, lnb_ref, sguw_ref, sgub_ref, cdw_ref, w_out_ref,
                o_ref, abuf, bbuf, dbuf, ybuf, *, ts, d_model):
    is_first = pl.program_id(1) == 0

    @pl.when(is_first)
    def _():
        abuf[pl.ds(0, A_HALO), :] = jnp.zeros((A_HALO, W_GROUP), F32)
        bbuf[pl.ds(0, B_HALO), :] = jnp.zeros((B_HALO, W_GROUP), F32)
        dbuf[pl.ds(0, D_HALO), :] = jnp.zeros((D_HALO, W_GROUP), F32)

    x = x_ref[0]
    mod = mod_ref[0]
    shift = mod[:, 0:d_model]
    scale = mod[:, d_model:2 * d_model]
    gate = mod[:, 2 * d_model:3 * d_model]
    h = _modulated_rms_norm(x, n1g_ref[...], scale, shift).astype(BF16)
    p = _dot(h, w_in_ref[...])

    def col(i):
        return p[:, i * W_GROUP:(i + 1) * W_GROUP]

    glu = col(0) * _sigmoid(col(1))
    ya = _conformer_conv(glu, abuf, caw_ref, cab_ref, gng_ref, gnb_ref, gmat_ref[...], ts)
    ybuf[:, 0:W_GROUP] = ya.astype(BF16)
    yb = _pool_mixer(col(2), bbuf, poolw_ref, pools_ref, ts, is_first)
    ybuf[:, W_GROUP:2 * W_GROUP] = yb.astype(BF16)
    yc = _spatial_gating(col(3), col(4), lng_ref, lnb_ref, sguw_ref, sgub_ref, ts)
    ybuf[:, 2 * W_GROUP:3 * W_GROUP] = yc.astype(BF16)
    yd = _short_conv(col(5), col(6) * col(7), dbuf, cdw_ref, ts)
    ybuf[:, 3 * W_GROUP:4 * W_GROUP] = yd.astype(BF16)

    y = _dot(ybuf[...], w_out_ref[...])
    o_ref[0] = x + gate * y


def _whole(shape):
    zeros = (0,) * len(shape)
    return pl.BlockSpec(shape, lambda b, s: zeros, pipeline_mode=pl.Buffered(1))


def _mix_call(x, mod_l, n1g, w_in, caw, cab, gng, gnb, gmat, poolw, pools, lng, lnb, sguw, sgub,
              cdw, w_out):
    batch, seq, d_model = x.shape
    ts = SEQ_TILE
    params = (n1g, w_in, caw, cab, gng, gnb, gmat, poolw, pools, lng, lnb, sguw, sgub, cdw, w_out)
    x_spec = pl.BlockSpec((1, ts, d_model), lambda b, s: (b, s, 0))
    return pl.pallas_call(
        functools.partial(_mix_kernel, ts=ts, d_model=d_model),
        grid=(batch, seq // ts),
        in_specs=[x_spec, pl.BlockSpec((1, 1, N_MOD * d_model), lambda b, s: (b, 0, 0))]
        + [_whole(p.shape) for p in params],
        out_specs=x_spec,
        out_shape=jax.ShapeDtypeStruct(x.shape, x.dtype),
        scratch_shapes=[
            pltpu.VMEM((ts + A_HALO, W_GROUP), F32),
            pltpu.VMEM((ts + B_HALO, W_GROUP), F32),
            pltpu.VMEM((ts + D_HALO, W_GROUP), F32),
            pltpu.VMEM((ts, 4 * W_GROUP), BF16),
        ],
        compiler_params=pltpu.CompilerParams(
            dimension_semantics=("arbitrary", "arbitrary"),
            vmem_limit_bytes=VMEM_LIMIT_BYTES),
        name="mix",
    )(x, mod_l, *params)


def _ffn_kernel(x_ref, mod_ref, n2g_ref, wg_ref, wu_ref, fcw_ref, wd_ref, fing_ref, o_ref, gbuf,
                *, ts, d_model, d_ff, final_norm):
    @pl.when(pl.program_id(1) == 0)
    def _():
        gbuf[pl.ds(0, F_HALO), :] = jnp.zeros((F_HALO, d_ff), F32)

    x = x_ref[0]
    mod = mod_ref[0]
    shift = mod[:, 3 * d_model:4 * d_model]
    scale = mod[:, 4 * d_model:5 * d_model]
    gate = mod[:, 5 * d_model:6 * d_model]
    h = _modulated_rms_norm(x, n2g_ref[...], scale, shift).astype(BF16)

    g = _dot(h, wg_ref[...])
    gbuf[pl.ds(F_HALO, ts), :] = g
    conv = fcw_ref[FFN_CONV_WIDTH - 1:FFN_CONV_WIDTH, :] * g
    for k in range(FFN_CONV_WIDTH - 1):
        back = FFN_CONV_WIDTH - 1 - k
        conv = conv + fcw_ref[k:k + 1, :] * gbuf[pl.ds(F_HALO - back, ts), :]
    gbuf[pl.ds(0, F_HALO), :] = gbuf[pl.ds(ts, F_HALO), :]

    a = conv * _sigmoid(conv)
    u = _dot(h, wu_ref[...])
    f = _dot((a * u).astype(BF16), wd_ref[...])
    out = x + gate * f
    if final_norm:
        ms = jnp.mean(out * out, axis=-1, keepdims=True)
        out = out * lax.rsqrt(ms + EPS) * fing_ref[...]
    o_ref[0] = out


def _ffn_call(x, mod_l, n2g, wg, wu, fcw, wd, fing, final_norm):
    batch, seq, d_model = x.shape
    d_ff = wg.shape[1]
    ts = SEQ_TILE
    params = (n2g, wg, wu, fcw, wd, fing)
    x_spec = pl.BlockSpec((1, ts, d_model), lambda b, s: (b, s, 0))
    return pl.pallas_call(
        functools.partial(_ffn_kernel, ts=ts, d_model=d_model, d_ff=d_ff, final_norm=final_norm),
        grid=(batch, seq // ts),
        in_specs=[x_spec, pl.BlockSpec((1, 1, N_MOD * d_model), lambda b, s: (b, 0, 0))]
        + [_whole(p.shape) for p in params],
        out_specs=x_spec,
        out_shape=jax.ShapeDtypeStruct(x.shape, x.dtype),
        scratch_shapes=[pltpu.VMEM((ts + F_HALO, d_ff), F32)],
        compiler_params=pltpu.CompilerParams(
            dimension_semantics=("arbitrary", "arbitrary"),
            vmem_limit_bytes=VMEM_LIMIT_BYTES),
        name="ffn",
    )(x, mod_l, *params)


def _block_diag(blocks):
    g, n, _ = blocks.shape
    eye = jnp.eye(g, dtype=blocks.dtype)
    return jnp.einsum("gij,gh->gihj", blocks, eye).reshape(g * n, g * n)


def kernel(x, c, norm1_g, ada_w, ada_b, w_in, conv_a_w, conv_a_b, gn_a_g, gn_a_b, pool_w, pool_scale, sgu_ln_g, sgu_ln_b, sgu_w, sgu_b, conv_d_w, w_out, norm2_g, ffn_w_gate, ffn_w_up, ffn_conv_w, ffn_w_down, final_g):
    batch, seq, d_model = x.shape
    n_layers = ada_w.shape[0]
    assert seq % SEQ_TILE == 0 and SEQ_TILE % CHUNK == 0 and SEQ_TILE % CONV_ROW_BLOCK == 0
    assert ada_w.shape[2] % MOD_COL_BLOCK == 0

    c_pad = jnp.pad(c, ((0, -batch % SUBLANES), (0, 0)))
    mod = _mod_call(c_pad, ada_w, ada_b)[:, :batch].reshape(n_layers, batch, 1, N_MOD * d_model)

    row = lambda v: v.reshape(1, -1)
    head = jnp.arange(W_GROUP) // HEAD_DIM
    gmat = (head[:, None] == head[None, :]).astype(BF16)
    for l in range(n_layers):
        sguw = jnp.transpose(sgu_w[l], (1, 0, 2)).reshape(CHUNK, N_HEADS * CHUNK)
        sgub = jnp.repeat(jnp.transpose(sgu_b[l]), HEAD_DIM, axis=1)
        x = _mix_call(
            x, mod[l], row(norm1_g[l]), w_in[l].astype(BF16), conv_a_w[l], row(conv_a_b[l]),
            row(gn_a_g[l]), row(gn_a_b[l]), gmat, _block_diag(pool_w[l]).astype(BF16),
            row(pool_scale[l]), row(sgu_ln_g[l]), row(sgu_ln_b[l]), sguw, sgub, conv_d_w[l],
            w_out[l].astype(BF16))
        x = _ffn_call(
            x, mod[l], row(norm2_g[l]), ffn_w_gate[l].astype(BF16), ffn_w_up[l].astype(BF16),
            ffn_conv_w[l], ffn_w_down[l].astype(BF16), row(final_g),
            final_norm=(l == n_layers - 1))
    return x
```

```python
import functools

import jax
import jax.numpy as jnp
from jax import lax
from jax.experimental import pallas as pl
from jax.experimental.pallas import tpu as pltpu

EPS = 1e-6
LOG2_E = 1.4426950408889634
N_MOD = 6
W_GROUP = 256
HEAD_DIM = 64
N_HEADS = W_GROUP // HEAD_DIM
CHUNK = 128
CONV_A_WIDTH = 31
POOL_WINDOWS = (2, 4, 8, 16)
SHORT_CONV_WIDTH = 3
FFN_CONV_WIDTH = 3

SUBLANES = 8
A_HALO = 32
B_HALO = 16
D_HALO = 8
F_HALO = 8
CONV_ROW_BLOCK = 64

SEQ_TILE = 512
MOD_COL_BLOCK = 2048
VMEM_LIMIT_BYTES = 56 * 1024 * 1024

F32 = jnp.float32
BF16 = jnp.bfloat16


def _dot(a, b):
    return jnp.dot(a, b, preferred_element_type=F32)


def _sigmoid(v):
    return 1.0 / (1.0 + jnp.exp2(v * (-LOG2_E)))


def _modulated_rms_norm(x, gain, scale, shift):
    ms = jnp.mean(x * x, axis=-1, keepdims=True)
    return (x * lax.rsqrt(ms + EPS)) * (gain * (1.0 + scale)) + shift


def _mod_kernel(c_ref, w_ref, b_ref, o_ref):
    c = c_ref[...]
    ca = c * _sigmoid(c)
    o_ref[0] = jnp.dot(ca, w_ref[0], preferred_element_type=F32,
                       precision=lax.Precision.HIGHEST) + b_ref[0]


def _mod_call(c_pad, ada_w, ada_b):
    n_layers, d_model, n_out = ada_w.shape
    rows = c_pad.shape[0]
    nb = MOD_COL_BLOCK
    return pl.pallas_call(
        _mod_kernel,
        grid=(n_layers, n_out // nb),
        in_specs=[
            pl.BlockSpec((rows, d_model), lambda l, j: (0, 0)),
            pl.BlockSpec((1, d_model, nb), lambda l, j: (l, 0, j)),
            pl.BlockSpec((1, 1, nb), lambda l, j: (l, 0, j)),
        ],
        out_specs=pl.BlockSpec((1, rows, nb), lambda l, j: (l, 0, j)),
        out_shape=jax.ShapeDtypeStruct((n_layers, rows, n_out), F32),
        compiler_params=pltpu.CompilerParams(
            dimension_semantics=("arbitrary", "arbitrary"),
            vmem_limit_bytes=VMEM_LIMIT_BYTES),
        name="mod",
    )(c_pad, ada_w, ada_b.reshape(n_layers, 1, n_out))


def _causal_conv31(glu, abuf, caw_ref, cab_ref, ts):
    abuf[pl.ds(A_HALO, ts), :] = glu
    rb = CONV_ROW_BLOCK
    first_off = A_HALO - (CONV_A_WIDTH - 1)
    blocks = []
    for r0 in range(0, ts, rb):
        acc = None
        for b in range(SUBLANES):
            taps = [(o - first_off, o - b) for o in range(first_off, A_HALO + 1)
                    if o % SUBLANES == b]
            span = max(off for _, off in taps)
            win = abuf[pl.ds(r0 + b, rb + span), :]
            part = None
            for k, off in taps:
                term = caw_ref[k:k + 1, :] * win[off:off + rb]
                part = term if part is None else part + term
            acc = part if acc is None else acc + part
        blocks.append(acc)
    conv = jnp.concatenate(blocks, axis=0) + cab_ref[...]
    abuf[pl.ds(0, A_HALO), :] = abuf[pl.ds(ts, A_HALO), :]
    return conv


def _group_norm_silu(conv, gng_ref, gnb_ref, gmat):
    inv = 1.0 / HEAD_DIM
    conv_hi = conv.astype(BF16)
    conv_lo = (conv - conv_hi.astype(F32)).astype(BF16)
    mu = (_dot(conv_hi, gmat) + _dot(conv_lo, gmat)) * inv
    d = conv - mu
    var = _dot((d * d).astype(BF16), gmat) * inv
    y = d * lax.rsqrt(var + EPS) * gng_ref[...] + gnb_ref[...]
    return y * _sigmoid(y)


def _pool_mixer(pb, bbuf, poolw_ref, pools_ref, ts, is_first):
    bbuf[pl.ds(B_HALO, ts), :] = pb
    half = W_GROUP // 2
    ext = bbuf[pl.ds(0, ts + B_HALO), :]
    s2 = ext + pltpu.roll(ext, 1, 0)
    s4 = s2 + pltpu.roll(s2, 2, 0)
    s4_hi = s4[:, half:W_GROUP]
    s8 = s4_hi + pltpu.roll(s4_hi, 4, 0)
    s16 = s8[B_HALO:] + s8[B_HALO - 8:B_HALO - 8 + ts]
    bbuf[pl.ds(0, B_HALO), :] = bbuf[pl.ds(ts, B_HALO), :]

    low = lax.broadcasted_iota(jnp.int32, (1, half), 1) < (half // 2)
    sum_a = jnp.where(low, s2[B_HALO:, 0:half], s4[B_HALO:, 0:half])
    sum_b = jnp.where(low, s8[B_HALO:], s16)
    sums = jnp.concatenate([sum_a, sum_b], axis=-1)

    lane_full = lax.broadcasted_iota(jnp.int32, (1, W_GROUP), 1)
    win = jnp.where(lane_full < 64, 2.0,
                    jnp.where(lane_full < 128, 4.0, jnp.where(lane_full < 192, 8.0, 16.0)))
    pos = lax.broadcasted_iota(jnp.int32, (B_HALO, W_GROUP), 0).astype(F32) + 1.0
    count = jnp.where(is_first, jnp.minimum(pos, win), win)
    mean = jnp.concatenate([sums[0:B_HALO] / count, sums[B_HALO:] * (1.0 / win)], axis=0)
    y = mean - pb
    return _dot(y.astype(BF16), poolw_ref[...]) * pools_ref[...]


def _spatial_gating(u, v, lng_ref, lnb_ref, sguw_ref, sgub_ref, ts):
    mu = jnp.mean(v, axis=-1, keepdims=True)
    d = v - mu
    var = jnp.mean(d * d, axis=-1, keepdims=True)
    vn = (d * lax.rsqrt(var + EPS) * lng_ref[...] + lnb_ref[...]).astype(BF16)

    row = lax.broadcasted_iota(jnp.int32, (CHUNK, N_HEADS * CHUNK), 0)
    col = lax.broadcasted_iota(jnp.int32, (CHUNK, N_HEADS * CHUNK), 1)
    wcat = jnp.where((col & (CHUNK - 1)) <= row, sguw_ref[...], 0.0).astype(BF16)

    head_of_lane = lax.broadcasted_iota(jnp.int32, (CHUNK, W_GROUP), 1) // HEAD_DIM
    zero = jnp.zeros((CHUNK, W_GROUP), BF16)
    outs = []
    for c0 in range(0, ts, CHUNK):
        vc = vn[c0:c0 + CHUNK]
        vstack = jnp.concatenate(
            [jnp.where(head_of_lane == h, vc, zero) for h in range(N_HEADS)], axis=0)
        outs.append(_dot(wcat, vstack) + sgub_ref[...])
    return u * jnp.concatenate(outs, axis=0)


def _short_conv(bg, cgh, dbuf, cdw_ref, ts):
    dbuf[pl.ds(D_HALO, ts), :] = cgh
    conv = cdw_ref[SHORT_CONV_WIDTH - 1:SHORT_CONV_WIDTH, :] * cgh
    for k in range(SHORT_CONV_WIDTH - 1):
        back = SHORT_CONV_WIDTH - 1 - k
        conv = conv + cdw_ref[k:k + 1, :] * dbuf[pl.ds(D_HALO - back, ts), :]
    dbuf[pl.ds(0, D_HALO), :] = dbuf[pl.ds(ts, D_HALO), :]
    return bg * conv


def _mix_kernel(xc_ref, xp_ref, modc_ref, modp_ref, n1g_ref, w_in_ref, caw_ref, cab_ref, gng_ref,
                gnb_ref, gmat_ref, poolw_ref, pools_ref, lng_ref, lnb_ref, sguw_ref, sgub_ref,
                cdw_ref, w_out_ref, o_ref, abuf, bbuf, dbuf, ybuf, *, ts, d_model, tiles_per_seq):
    step = pl.program_id(0)
    is_first = lax.rem(step, tiles_per_seq) == 0

    @pl.when(step == 0)
    def _():
        ybuf[...] = jnp.zeros(ybuf.shape, BF16)

    @pl.when(is_first)
    def _():
        abuf[pl.ds(0, A_HALO), :] = jnp.zeros((A_HALO, W_GROUP), F32)
        bbuf[pl.ds(0, B_HALO), :] = jnp.zeros((B_HALO, W_GROUP), F32)
        dbuf[pl.ds(0, D_HALO), :] = jnp.zeros((D_HALO, W_GROUP), F32)

    mod = modc_ref[0]
    shift = mod[:, 0:d_model]
    scale = mod[:, d_model:2 * d_model]
    h = _modulated_rms_norm(xc_ref[0], n1g_ref[...], scale, shift).astype(BF16)

    def proj(first, n):
        return _dot(h, w_in_ref[0, :, first * W_GROUP:(first + n) * W_GROUP])

    def col(p, i):
        return p[:, i * W_GROUP:(i + 1) * W_GROUP]

    pa = proj(0, 2)
    conv = _causal_conv31(col(pa, 0) * _sigmoid(col(pa, 1)), abuf, caw_ref, cab_ref, ts)
    pb = proj(2, 1)
    pc = proj(3, 2)
    pd = proj(5, 3)

    gate = modp_ref[0][:, 2 * d_model:3 * d_model]
    o_ref[0] = xp_ref[0] + gate * _dot(ybuf[...], w_out_ref[0])

    ya = _group_norm_silu(conv, gng_ref, gnb_ref, gmat_ref[...])
    ybuf[:, 0:W_GROUP] = ya.astype(BF16)
    yb = _pool_mixer(pb, bbuf, poolw_ref, pools_ref, ts, is_first)
    ybuf[:, W_GROUP:2 * W_GROUP] = yb.astype(BF16)
    yc = _spatial_gating(col(pc, 0), col(pc, 1), lng_ref, lnb_ref, sguw_ref, sgub_ref, ts)
    ybuf[:, 2 * W_GROUP:3 * W_GROUP] = yc.astype(BF16)
    yd = _short_conv(col(pd, 0), col(pd, 1) * col(pd, 2), dbuf, cdw_ref, ts)
    ybuf[:, 3 * W_GROUP:4 * W_GROUP] = yd.astype(BF16)


def _whole(shape):
    zeros = (0,) * len(shape)
    return pl.BlockSpec(shape, lambda *_: zeros, pipeline_mode=pl.Buffered(1))


def _layer(stacked, layer):
    index = (layer,) + (0,) * (stacked.ndim - 1)
    return pl.BlockSpec((1,) + stacked.shape[1:], lambda *_: index, pipeline_mode=pl.Buffered(1))


def _mix_call(x, mod_l, layer, w_in, w_out, n1g, caw, cab, gng, gnb, gmat, poolw, pools, lng, lnb,
              sguw, sgub, cdw):
    batch, seq, d_model = x.shape
    ts = SEQ_TILE
    tiles_per_seq = seq // ts
    n_tiles = batch * tiles_per_seq

    def cur(i):
        return jnp.minimum(i, n_tiles - 1)

    def prev(i):
        return jnp.maximum(i - 1, 0)

    def x_spec(tile):
        return pl.BlockSpec((1, ts, d_model),
                            lambda i: (tile(i) // tiles_per_seq, tile(i) % tiles_per_seq, 0))

    def mod_spec(tile):
        return pl.BlockSpec((1, 1, N_MOD * d_model), lambda i: (tile(i) // tiles_per_seq, 0, 0))

    small = (caw, cab, gng, gnb, gmat, poolw, pools, lng, lnb, sguw, sgub, cdw)
    return pl.pallas_call(
        functools.partial(_mix_kernel, ts=ts, d_model=d_model, tiles_per_seq=tiles_per_seq),
        grid=(n_tiles + 1,),
        in_specs=[x_spec(cur), x_spec(prev), mod_spec(cur), mod_spec(prev), _whole(n1g.shape),
                  _layer(w_in, layer)] + [_whole(p.shape) for p in small] + [_layer(w_out, layer)],
        out_specs=x_spec(prev),
        out_shape=jax.ShapeDtypeStruct(x.shape, x.dtype),
        scratch_shapes=[
            pltpu.VMEM((ts + A_HALO, W_GROUP), F32),
            pltpu.VMEM((ts + B_HALO, W_GROUP), F32),
            pltpu.VMEM((ts + D_HALO, W_GROUP), F32),
            pltpu.VMEM((ts, 4 * W_GROUP), BF16),
        ],
        compiler_params=pltpu.CompilerParams(
            dimension_semantics=("arbitrary",),
            vmem_limit_bytes=VMEM_LIMIT_BYTES),
        name="mix",
    )(x, x, mod_l, mod_l, n1g, w_in, *small, w_out)


def _ffn_kernel(x_ref, mod_ref, n2g_ref, wg_ref, wu_ref, fcw_ref, wd_ref, fing_ref, o_ref, gbuf,
                *, ts, d_model, d_ff, final_norm):
    @pl.when(pl.program_id(1) == 0)
    def _():
        gbuf[pl.ds(0, F_HALO), :] = jnp.zeros((F_HALO, d_ff), F32)

    x = x_ref[0]
    mod = mod_ref[0]
    shift = mod[:, 3 * d_model:4 * d_model]
    scale = mod[:, 4 * d_model:5 * d_model]
    gate = mod[:, 5 * d_model:6 * d_model]
    h = _modulated_rms_norm(x, n2g_ref[...], scale, shift).astype(BF16)

    g = _dot(h, wg_ref[0])
    gbuf[pl.ds(F_HALO, ts), :] = g
    conv = fcw_ref[FFN_CONV_WIDTH - 1:FFN_CONV_WIDTH, :] * g
    for k in range(FFN_CONV_WIDTH - 1):
        back = FFN_CONV_WIDTH - 1 - k
        conv = conv + fcw_ref[k:k + 1, :] * gbuf[pl.ds(F_HALO - back, ts), :]
    gbuf[pl.ds(0, F_HALO), :] = gbuf[pl.ds(ts, F_HALO), :]

    a = conv * _sigmoid(conv)
    u = _dot(h, wu_ref[0])
    f = _dot((a * u).astype(BF16), wd_ref[0])
    out = x + gate * f
    if final_norm:
        ms = jnp.mean(out * out, axis=-1, keepdims=True)
        out = out * lax.rsqrt(ms + EPS) * fing_ref[...]
    o_ref[0] = out


def _ffn_call(x, mod_l, layer, wg, wu, wd, n2g, fcw, fing, final_norm):
    batch, seq, d_model = x.shape
    d_ff = wg.shape[2]
    ts = SEQ_TILE
    x_spec = pl.BlockSpec((1, ts, d_model), lambda b, s: (b, s, 0))
    return pl.pallas_call(
        functools.partial(_ffn_kernel, ts=ts, d_model=d_model, d_ff=d_ff, final_norm=final_norm),
        grid=(batch, seq // ts),
        in_specs=[x_spec, pl.BlockSpec((1, 1, N_MOD * d_model), lambda b, s: (b, 0, 0)),
                  _whole(n2g.shape), _layer(wg, layer), _layer(wu, layer), _whole(fcw.shape),
                  _layer(wd, layer), _whole(fing.shape)],
        out_specs=x_spec,
        out_shape=jax.ShapeDtypeStruct(x.shape, x.dtype),
        scratch_shapes=[pltpu.VMEM((ts + F_HALO, d_ff), F32)],
        compiler_params=pltpu.CompilerParams(
            dimension_semantics=("arbitrary", "arbitrary"),
            vmem_limit_bytes=VMEM_LIMIT_BYTES),
        name="ffn",
    )(x, mod_l, n2g, wg, wu, fcw, wd, fing)


def _block_diag(blocks):
    g, n, _ = blocks.shape
    eye = jnp.eye(g, dtype=blocks.dtype)
    return jnp.einsum("gij,gh->gihj", blocks, eye).reshape(g * n, g * n)


def kernel(x, c, norm1_g, ada_w, ada_b, w_in, conv_a_w, conv_a_b, gn_a_g, gn_a_b, pool_w, pool_scale, sgu_ln_g, sgu_ln_b, sgu_w, sgu_b, conv_d_w, w_out, norm2_g, ffn_w_gate, ffn_w_up, ffn_conv_w, ffn_w_down, final_g):
    batch, seq, d_model = x.shape
    n_layers = ada_w.shape[0]
    assert seq % SEQ_TILE == 0 and SEQ_TILE % CHUNK == 0 and SEQ_TILE % CONV_ROW_BLOCK == 0
    assert ada_w.shape[2] % MOD_COL_BLOCK == 0

    c_pad = jnp.pad(c, ((0, -batch % SUBLANES), (0, 0)))
    mod = _mod_call(c_pad, ada_w, ada_b)[:, :batch].reshape(n_layers, batch, 1, N_MOD * d_model)

    row = lambda v: v.reshape(1, -1)
    head = jnp.arange(W_GROUP) // HEAD_DIM
    gmat = (head[:, None] == head[None, :]).astype(BF16)
    w_in_b, w_out_b = w_in.astype(BF16), w_out.astype(BF16)
    wg_b, wu_b, wd_b = ffn_w_gate.astype(BF16), ffn_w_up.astype(BF16), ffn_w_down.astype(BF16)
    for l in range(n_layers):
        sguw = jnp.transpose(sgu_w[l], (1, 0, 2)).reshape(CHUNK, N_HEADS * CHUNK)
        sgub = jnp.repeat(jnp.transpose(sgu_b[l]), HEAD_DIM, axis=1)
        x = _mix_call(
            x, mod[l], l, w_in_b, w_out_b, row(norm1_g[l]), conv_a_w[l], row(conv_a_b[l]),
            row(gn_a_g[l]), row(gn_a_b[l]), gmat, _block_diag(pool_w[l]).astype(BF16),
            row(pool_scale[l]), row(sgu_ln_g[l]), row(sgu_ln_b[l]), sguw, sgub, conv_d_w[l])
        x = _ffn_call(
            x, mod[l], l, wg_b, wu_b, wd_b, row(norm2_g[l]), ffn_conv_w[l], row(final_g),
            final_norm=(l == n_layers - 1))
    return x
```

```python
import functools

import jax
import jax.numpy as jnp
from jax import lax
from jax.experimental import pallas as pl
from jax.experimental.pallas import tpu as pltpu

EPS = 1e-6
LOG2_E = 1.4426950408889634
N_MOD = 6
W_GROUP = 256
HEAD_DIM = 64
N_HEADS = W_GROUP // HEAD_DIM
CHUNK = 128
CONV_A_WIDTH = 31
POOL_WINDOWS = (2, 4, 8, 16)
SHORT_CONV_WIDTH = 3
FFN_CONV_WIDTH = 3

SUBLANES = 8
A_HALO = 32
B_HALO = 16
D_HALO = 8
F_HALO = 8
CONV_ROW_BLOCK = 64

SEQ_TILE = 512
MOD_COL_BLOCK = 2048
FFN_COL_BLOCK = 512
VMEM_LIMIT_BYTES = 56 * 1024 * 1024

F32 = jnp.float32
BF16 = jnp.bfloat16


def _dot(a, b):
    return jnp.dot(a, b, preferred_element_type=F32)


def _sigmoid(v):
    return 1.0 / (1.0 + jnp.exp2(v * (-LOG2_E)))


def _modulated_rms_norm(x, gain, scale, shift):
    ms = jnp.mean(x * x, axis=-1, keepdims=True)
    return (x * lax.rsqrt(ms + EPS)) * (gain * (1.0 + scale)) + shift


def _mod_kernel(c_ref, w_ref, b_ref, o_ref):
    c = c_ref[...]
    ca = c * _sigmoid(c)
    o_ref[0] = jnp.dot(ca, w_ref[0], preferred_element_type=F32,
                       precision=lax.Precision.HIGHEST) + b_ref[0]


def _mod_call(c_pad, ada_w, ada_b):
    n_layers, d_model, n_out = ada_w.shape
    rows = c_pad.shape[0]
    nb = MOD_COL_BLOCK
    return pl.pallas_call(
        _mod_kernel,
        grid=(n_layers, n_out // nb),
        in_specs=[
            pl.BlockSpec((rows, d_model), lambda l, j: (0, 0)),
            pl.BlockSpec((1, d_model, nb), lambda l, j: (l, 0, j)),
            pl.BlockSpec((1, 1, nb), lambda l, j: (l, 0, j)),
        ],
        out_specs=pl.BlockSpec((1, rows, nb), lambda l, j: (l, 0, j)),
        out_shape=jax.ShapeDtypeStruct((n_layers, rows, n_out), F32),
        compiler_params=pltpu.CompilerParams(
            dimension_semantics=("arbitrary", "arbitrary"),
            vmem_limit_bytes=VMEM_LIMIT_BYTES),
        name="mod",
    )(c_pad, ada_w, ada_b.reshape(n_layers, 1, n_out))


def _causal_conv31(glu, abuf, caw_ref, cab_ref, ts):
    abuf[pl.ds(A_HALO, ts), :] = glu
    rb = CONV_ROW_BLOCK
    first_off = A_HALO - (CONV_A_WIDTH - 1)
    blocks = []
    for r0 in range(0, ts, rb):
        acc = None
        for b in range(SUBLANES):
            taps = [(o - first_off, o - b) for o in range(first_off, A_HALO + 1)
                    if o % SUBLANES == b]
            span = max(off for _, off in taps)
            win = abuf[pl.ds(r0 + b, rb + span), :]
            part = None
            for k, off in taps:
                term = caw_ref[k:k + 1, :] * win[off:off + rb]
                part = term if part is None else part + term
            acc = part if acc is None else acc + part
        blocks.append(acc)
    conv = jnp.concatenate(blocks, axis=0) + cab_ref[...]
    abuf[pl.ds(0, A_HALO), :] = abuf[pl.ds(ts, A_HALO), :]
    return conv


def _group_norm_silu(conv, gng_ref, gnb_ref, gmat):
    inv = 1.0 / HEAD_DIM
    conv_hi = conv.astype(BF16)
    conv_lo = (conv - conv_hi.astype(F32)).astype(BF16)
    mu = (_dot(conv_hi, gmat) + _dot(conv_lo, gmat)) * inv
    d = conv - mu
    var = _dot((d * d).astype(BF16), gmat) * inv
    y = d * lax.rsqrt(var + EPS) * gng_ref[...] + gnb_ref[...]
    return y * _sigmoid(y)


def _pool_mixer(pb, bbuf, poolw_ref, pools_ref, ts, is_first):
    bbuf[pl.ds(B_HALO, ts), :] = pb
    half = W_GROUP // 2
    ext = bbuf[pl.ds(0, ts + B_HALO), :]
    s2 = ext + pltpu.roll(ext, 1, 0)
    s4 = s2 + pltpu.roll(s2, 2, 0)
    s4_hi = s4[:, half:W_GROUP]
    s8 = s4_hi + pltpu.roll(s4_hi, 4, 0)
    s16 = s8[B_HALO:] + s8[B_HALO - 8:B_HALO - 8 + ts]
    bbuf[pl.ds(0, B_HALO), :] = bbuf[pl.ds(ts, B_HALO), :]

    low = lax.broadcasted_iota(jnp.int32, (1, half), 1) < (half // 2)
    sum_a = jnp.where(low, s2[B_HALO:, 0:half], s4[B_HALO:, 0:half])
    sum_b = jnp.where(low, s8[B_HALO:], s16)
    sums = jnp.concatenate([sum_a, sum_b], axis=-1)

    lane_full = lax.broadcasted_iota(jnp.int32, (1, W_GROUP), 1)
    win = jnp.where(lane_full < 64, 2.0,
                    jnp.where(lane_full < 128, 4.0, jnp.where(lane_full < 192, 8.0, 16.0)))
    pos = lax.broadcasted_iota(jnp.int32, (B_HALO, W_GROUP), 0).astype(F32) + 1.0
    count = jnp.where(is_first, jnp.minimum(pos, win), win)
    mean = jnp.concatenate([sums[0:B_HALO] / count, sums[B_HALO:] * (1.0 / win)], axis=0)
    y = mean - pb
    return _dot(y.astype(BF16), poolw_ref[...]) * pools_ref[...]


def _spatial_gating(u, v, lng_ref, lnb_ref, sguw_ref, sgub_ref, ts):
    mu = jnp.mean(v, axis=-1, keepdims=True)
    d = v - mu
    var = jnp.mean(d * d, axis=-1, keepdims=True)
    vn = (d * lax.rsqrt(var + EPS) * lng_ref[...] + lnb_ref[...]).astype(BF16)

    row = lax.broadcasted_iota(jnp.int32, (CHUNK, N_HEADS * CHUNK), 0)
    col = lax.broadcasted_iota(jnp.int32, (CHUNK, N_HEADS * CHUNK), 1)
    wcat = jnp.where((col & (CHUNK - 1)) <= row, sguw_ref[...], 0.0).astype(BF16)

    head_of_lane = lax.broadcasted_iota(jnp.int32, (CHUNK, W_GROUP), 1) // HEAD_DIM
    zero = jnp.zeros((CHUNK, W_GROUP), BF16)
    outs = []
    for c0 in range(0, ts, CHUNK):
        vc = vn[c0:c0 + CHUNK]
        vstack = jnp.concatenate(
            [jnp.where(head_of_lane == h, vc, zero) for h in range(N_HEADS)], axis=0)
        outs.append(_dot(wcat, vstack) + sgub_ref[...])
    return u * jnp.concatenate(outs, axis=0)


def _short_conv(bg, cgh, dbuf, cdw_ref, ts):
    dbuf[pl.ds(D_HALO, ts), :] = cgh
    conv = cdw_ref[SHORT_CONV_WIDTH - 1:SHORT_CONV_WIDTH, :] * cgh
    for k in range(SHORT_CONV_WIDTH - 1):
        back = SHORT_CONV_WIDTH - 1 - k
        conv = conv + cdw_ref[k:k + 1, :] * dbuf[pl.ds(D_HALO - back, ts), :]
    dbuf[pl.ds(0, D_HALO), :] = dbuf[pl.ds(ts, D_HALO), :]
    return bg * conv


def _ffn_gated(h, wg_ref, wu_ref, fcw_ref, gcarry, mbuf, ts, col_blocks):
    for c0, c1 in col_blocks:
        cols = slice(c0, c1)
        g = _dot(h, wg_ref[0, :, cols])
        u = _dot(h, wu_ref[0, :, cols])
        ext = jnp.concatenate([gcarry[:, cols], g], axis=0)
        conv = fcw_ref[FFN_CONV_WIDTH - 1:FFN_CONV_WIDTH, cols] * g
        for k in range(FFN_CONV_WIDTH - 1):
            back = FFN_CONV_WIDTH - 1 - k
            conv = conv + fcw_ref[k:k + 1, cols] * pltpu.roll(ext, back, 0)[F_HALO:]
        gcarry[:, cols] = g[ts - F_HALO:]
        mbuf[:, cols] = (conv * _sigmoid(conv) * u).astype(BF16)


def _layer_kernel(x_ref, modc_ref, modp_ref, n1g_ref, w_in_ref, caw_ref, cab_ref, gng_ref, gnb_ref,
                  gmat_ref, poolw_ref, pools_ref, lng_ref, lnb_ref, sguw_ref, sgub_ref, cdw_ref,
                  w_out_ref, n2g_ref, wg_ref, wu_ref, fcw_ref, wd_ref, fing_ref, o_ref,
                  abuf, bbuf, dbuf, ybuf, hbuf, xmid, gcarry, mbuf,
                  *, ts, d_model, d_ff, tiles_per_seq, final_norm):
    step = pl.program_id(0)
    mix_first = lax.rem(step, tiles_per_seq) == 0
    ffn_first = (step == 0) | (lax.rem(step - 1, tiles_per_seq) == 0)

    @pl.when(step == 0)
    def _():
        hbuf[...] = jnp.zeros(hbuf.shape, BF16)
        xmid[...] = jnp.zeros(xmid.shape, F32)

    @pl.when(mix_first)
    def _():
        abuf[pl.ds(0, A_HALO), :] = jnp.zeros((A_HALO, W_GROUP), F32)
        bbuf[pl.ds(0, B_HALO), :] = jnp.zeros((B_HALO, W_GROUP), F32)
        dbuf[pl.ds(0, D_HALO), :] = jnp.zeros((D_HALO, W_GROUP), F32)

    @pl.when(ffn_first)
    def _():
        gcarry[...] = jnp.zeros(gcarry.shape, F32)

    modc = modc_ref[0]
    modp = modp_ref[0]

    def part(mod, i):
        return mod[:, i * d_model:(i + 1) * d_model]

    x_res = xmid[...]
    col_blocks = [(c0, min(c0 + FFN_COL_BLOCK, d_ff)) for c0 in range(0, d_ff, FFN_COL_BLOCK)]
    _ffn_gated(hbuf[...], wg_ref, wu_ref, fcw_ref, gcarry, mbuf, ts, col_blocks)

    x = x_ref[0]
    h1 = _modulated_rms_norm(x, n1g_ref[...], part(modc, 1), part(modc, 0)).astype(BF16)

    def proj(first, n):
        return _dot(h1, w_in_ref[0, :, first * W_GROUP:(first + n) * W_GROUP])

    def col(p, i):
        return p[:, i * W_GROUP:(i + 1) * W_GROUP]

    pa = proj(0, 2)
    conv = _causal_conv31(col(pa, 0) * _sigmoid(col(pa, 1)), abuf, caw_ref, cab_ref, ts)
    pb = proj(2, 1)
    pc = proj(3, 2)
    pd = proj(5, 3)

    f = _dot(mbuf[...], wd_ref[0])

    ya = _group_norm_silu(conv, gng_ref, gnb_ref, gmat_ref[...])
    ybuf[:, 0:W_GROUP] = ya.astype(BF16)
    yb = _pool_mixer(pb, bbuf, poolw_ref, pools_ref, ts, mix_first)
    ybuf[:, W_GROUP:2 * W_GROUP] = yb.astype(BF16)
    yc = _spatial_gating(col(pc, 0), col(pc, 1), lng_ref, lnb_ref, sguw_ref, sgub_ref, ts)
    ybuf[:, 2 * W_GROUP:3 * W_GROUP] = yc.astype(BF16)
    yd = _short_conv(col(pd, 0), col(pd, 1) * col(pd, 2), dbuf, cdw_ref, ts)
    ybuf[:, 3 * W_GROUP:4 * W_GROUP] = yd.astype(BF16)
    x_mid = x + part(modc, 2) * _dot(ybuf[...], w_out_ref[0])
    xmid[...] = x_mid
    hbuf[...] = _modulated_rms_norm(x_mid, n2g_ref[...], part(modc, 4), part(modc, 3)).astype(BF16)

    out = x_res + part(modp, 5) * f
    if final_norm:
        ms = jnp.mean(out * out, axis=-1, keepdims=True)
        out = out * lax.rsqrt(ms + EPS) * fing_ref[...]
    o_ref[0] = out


def _whole(shape):
    zeros = (0,) * len(shape)
    return pl.BlockSpec(shape, lambda *_: zeros, pipeline_mode=pl.Buffered(1))


def _layer(stacked, layer):
    index = (layer,) + (0,) * (stacked.ndim - 1)
    return pl.BlockSpec((1,) + stacked.shape[1:], lambda *_: index, pipeline_mode=pl.Buffered(1))


def _layer_call(x, mod_l, layer, final_norm, w_in, w_out, wg, wu, wd, n1g, caw, cab, gng, gnb, gmat,
                poolw, pools, lng, lnb, sguw, sgub, cdw, n2g, fcw, fing):
    batch, seq, d_model = x.shape
    d_ff = wg.shape[2]
    ts = SEQ_TILE
    tiles_per_seq = seq // ts
    n_tiles = batch * tiles_per_seq

    def cur(i):
        return jnp.minimum(i, n_tiles - 1)

    def prev(i):
        return jnp.maximum(i - 1, 0)

    def x_spec(tile):
        return pl.BlockSpec((1, ts, d_model),
                            lambda i: (tile(i) // tiles_per_seq, tile(i) % tiles_per_seq, 0))

    def mod_spec(tile):
        return pl.BlockSpec((1, 1, N_MOD * d_model), lambda i: (tile(i) // tiles_per_seq, 0, 0))

    mix_small = (caw, cab, gng, gnb, gmat, poolw, pools, lng, lnb, sguw, sgub, cdw)
    return pl.pallas_call(
        functools.partial(_layer_kernel, ts=ts, d_model=d_model, d_ff=d_ff,
                          tiles_per_seq=tiles_per_seq, final_norm=final_norm),
        grid=(n_tiles + 1,),
        in_specs=[x_spec(cur), mod_spec(cur), mod_spec(prev), _whole(n1g.shape), _layer(w_in, layer)]
        + [_whole(p.shape) for p in mix_small]
        + [_layer(w_out, layer), _whole(n2g.shape), _layer(wg, layer), _layer(wu, layer),
           _whole(fcw.shape), _layer(wd, layer), _whole(fing.shape)],
        out_specs=x_spec(prev),
        out_shape=jax.ShapeDtypeStruct(x.shape, x.dtype),
        scratch_shapes=[
            pltpu.VMEM((ts + A_HALO, W_GROUP), F32),
            pltpu.VMEM((ts + B_HALO, W_GROUP), F32),
            pltpu.VMEM((ts + D_HALO, W_GROUP), F32),
            pltpu.VMEM((ts, 4 * W_GROUP), BF16),
            pltpu.VMEM((ts, d_model), BF16),
            pltpu.VMEM((ts, d_model), F32),
            pltpu.VMEM((F_HALO, d_ff), F32),
            pltpu.VMEM((ts, d_ff), BF16),
        ],
        compiler_params=pltpu.CompilerParams(
            dimension_semantics=("arbitrary",),
            vmem_limit_bytes=VMEM_LIMIT_BYTES),
        name="layer",
    )(x, mod_l, mod_l, n1g, w_in, *mix_small, w_out, n2g, wg, wu, fcw, wd, fing)


def _block_diag(blocks):
    g, n, _ = blocks.shape
    eye = jnp.eye(g, dtype=blocks.dtype)
    return jnp.einsum("gij,gh->gihj", blocks, eye).reshape(g * n, g * n)


def kernel(x, c, norm1_g, ada_w, ada_b, w_in, conv_a_w, conv_a_b, gn_a_g, gn_a_b, pool_w, pool_scale, sgu_ln_g, sgu_ln_b, sgu_w, sgu_b, conv_d_w, w_out, norm2_g, ffn_w_gate, ffn_w_up, ffn_conv_w, ffn_w_down, final_g):
    batch, seq, d_model = x.shape
    n_layers = ada_w.shape[0]
    assert seq % SEQ_TILE == 0 and SEQ_TILE % CHUNK == 0 and SEQ_TILE % CONV_ROW_BLOCK == 0
    assert ada_w.shape[2] % MOD_COL_BLOCK == 0

    c_pad = jnp.pad(c, ((0, -batch % SUBLANES), (0, 0)))
    mod = _mod_call(c_pad, ada_w, ada_b)[:, :batch].reshape(n_layers, batch, 1, N_MOD * d_model)

    row = lambda v: v.reshape(1, -1)
    head = jnp.arange(W_GROUP) // HEAD_DIM
    gmat = (head[:, None] == head[None, :]).astype(BF16)
    w_in_b, w_out_b = w_in.astype(BF16), w_out.astype(BF16)
    wg_b, wu_b, wd_b = ffn_w_gate.astype(BF16), ffn_w_up.astype(BF16), ffn_w_down.astype(BF16)
    for l in range(n_layers):
        sguw = jnp.transpose(sgu_w[l], (1, 0, 2)).reshape(CHUNK, N_HEADS * CHUNK)
        sgub = jnp.repeat(jnp.transpose(sgu_b[l]), HEAD_DIM, axis=1)
        x = _layer_call(
            x, mod[l], l, l == n_layers - 1, w_in_b, w_out_b, wg_b, wu_b, wd_b, row(norm1_g[l]),
            conv_a_w[l], row(conv_a_b[l]), row(gn_a_g[l]), row(gn_a_b[l]), gmat,
            _block_diag(pool_w[l]).astype(BF16), row(pool_scale[l]), row(sgu_ln_g[l]),
            row(sgu_ln_b[l]), sguw, sgub, conv_d_w[l], row(norm2_g[l]), ffn_conv_w[l],
            row(final_g))
    return x
```

```python
import functools

import jax
import jax.numpy as jnp
from jax import lax
from jax.experimental import pallas as pl
from jax.experimental.pallas import tpu as pltpu

EPS = 1e-6
LOG2_E = 1.4426950408889634
N_MOD = 6
W_GROUP = 256
HEAD_DIM = 64
N_HEADS = W_GROUP // HEAD_DIM
CHUNK = 128
CONV_A_WIDTH = 31
POOL_WINDOWS = (2, 4, 8, 16)
SHORT_CONV_WIDTH = 3
FFN_CONV_WIDTH = 3

SUBLANES = 8
A_HALO = 32
B_HALO = 16
D_HALO = 8
F_HALO = 8
CONV_ROW_BLOCK = 64

SEQ_TILE = 512
MOD_COL_BLOCK = 2048
FFN_COL_BLOCK = 512
VMEM_LIMIT_BYTES = 56 * 1024 * 1024

F32 = jnp.float32
BF16 = jnp.bfloat16


def _dot(a, b):
    return jnp.dot(a, b, preferred_element_type=F32)


def _sigmoid(v):
    return 1.0 / (1.0 + jnp.exp2(v * (-LOG2_E)))


def _modulated_rms_norm(x, gain, scale, shift):
    ms = jnp.mean(x * x, axis=-1, keepdims=True)
    return (x * lax.rsqrt(ms + EPS)) * (gain * (1.0 + scale)) + shift


def _mod_kernel(c_ref, w_ref, b_ref, o_ref):
    c = c_ref[...]
    ca = c * _sigmoid(c)
    o_ref[0] = _dot(ca.astype(BF16), w_ref[0].astype(BF16)) + b_ref[0]


def _mod_call(c_pad, ada_w, ada_b):
    n_layers, d_model, n_out = ada_w.shape
    rows = c_pad.shape[0]
    nb = MOD_COL_BLOCK
    return pl.pallas_call(
        _mod_kernel,
        grid=(n_layers, n_out // nb),
        in_specs=[
            pl.BlockSpec((rows, d_model), lambda l, j: (0, 0)),
            pl.BlockSpec((1, d_model, nb), lambda l, j: (l, 0, j)),
            pl.BlockSpec((1, 1, nb), lambda l, j: (l, 0, j)),
        ],
        out_specs=pl.BlockSpec((1, rows, nb), lambda l, j: (l, 0, j)),
        out_shape=jax.ShapeDtypeStruct((n_layers, rows, n_out), F32),
        compiler_params=pltpu.CompilerParams(
            dimension_semantics=("arbitrary", "arbitrary"),
            vmem_limit_bytes=VMEM_LIMIT_BYTES),
        name="mod",
    )(c_pad, ada_w, ada_b.reshape(n_layers, 1, n_out))


def _causal_conv31(glu, abuf, caw_ref, cab_ref, ts):
    abuf[pl.ds(A_HALO, ts), :] = glu
    rb = CONV_ROW_BLOCK
    first_off = A_HALO - (CONV_A_WIDTH - 1)
    blocks = []
    for r0 in range(0, ts, rb):
        acc = None
        for b in range(SUBLANES):
            taps = [(o - first_off, o - b) for o in range(first_off, A_HALO + 1)
                    if o % SUBLANES == b]
            span = max(off for _, off in taps)
            win = abuf[pl.ds(r0 + b, rb + span), :]
            part = None
            for k, off in taps:
                term = caw_ref[k:k + 1, :] * win[off:off + rb]
                part = term if part is None else part + term
            acc = part if acc is None else acc + part
        blocks.append(acc)
    conv = jnp.concatenate(blocks, axis=0) + cab_ref[...]
    abuf[pl.ds(0, A_HALO), :] = abuf[pl.ds(ts, A_HALO), :]
    return conv


def _group_norm_silu(conv, gng_ref, gnb_ref, gmat):
    inv = 1.0 / HEAD_DIM
    conv_hi = conv.astype(BF16)
    conv_lo = (conv - conv_hi.astype(F32)).astype(BF16)
    mu = (_dot(conv_hi, gmat) + _dot(conv_lo, gmat)) * inv
    d = conv - mu
    var = _dot((d * d).astype(BF16), gmat) * inv
    y = d * lax.rsqrt(var + EPS) * gng_ref[...] + gnb_ref[...]
    return y * _sigmoid(y)


def _pool_mixer(pb, bbuf, poolw_ref, pools_ref, ts, is_first):
    bbuf[pl.ds(B_HALO, ts), :] = pb
    half = W_GROUP // 2
    ext = bbuf[pl.ds(0, ts + B_HALO), :]
    s2 = ext + pltpu.roll(ext, 1, 0)
    s4 = s2 + pltpu.roll(s2, 2, 0)
    s4_hi = s4[:, half:W_GROUP]
    s8 = s4_hi + pltpu.roll(s4_hi, 4, 0)
    s16 = s8[B_HALO:] + s8[B_HALO - 8:B_HALO - 8 + ts]
    bbuf[pl.ds(0, B_HALO), :] = bbuf[pl.ds(ts, B_HALO), :]

    low = lax.broadcasted_iota(jnp.int32, (1, half), 1) < (half // 2)
    sum_a = jnp.where(low, s2[B_HALO:, 0:half], s4[B_HALO:, 0:half])
    sum_b = jnp.where(low, s8[B_HALO:], s16)
    sums = jnp.concatenate([sum_a, sum_b], axis=-1)

    lane_full = lax.broadcasted_iota(jnp.int32, (1, W_GROUP), 1)
    win = jnp.where(lane_full < 64, 2.0,
                    jnp.where(lane_full < 128, 4.0, jnp.where(lane_full < 192, 8.0, 16.0)))
    pos = lax.broadcasted_iota(jnp.int32, (B_HALO, W_GROUP), 0).astype(F32) + 1.0
    count = jnp.where(is_first, jnp.minimum(pos, win), win)
    mean = jnp.concatenate([sums[0:B_HALO] / count, sums[B_HALO:] * (1.0 / win)], axis=0)
    y = mean - pb
    return _dot(y.astype(BF16), poolw_ref[...]) * pools_ref[...]


def _spatial_gating(u, v, lng_ref, lnb_ref, sguw_ref, sgub_ref, ts):
    mu = jnp.mean(v, axis=-1, keepdims=True)
    d = v - mu
    var = jnp.mean(d * d, axis=-1, keepdims=True)
    vn = (d * lax.rsqrt(var + EPS) * lng_ref[...] + lnb_ref[...]).astype(BF16)

    row = lax.broadcasted_iota(jnp.int32, (CHUNK, N_HEADS * CHUNK), 0)
    col = lax.broadcasted_iota(jnp.int32, (CHUNK, N_HEADS * CHUNK), 1)
    wcat = jnp.where((col & (CHUNK - 1)) <= row, sguw_ref[...], 0.0).astype(BF16)

    head_of_lane = lax.broadcasted_iota(jnp.int32, (CHUNK, W_GROUP), 1) // HEAD_DIM
    zero = jnp.zeros((CHUNK, W_GROUP), BF16)
    outs = []
    for c0 in range(0, ts, CHUNK):
        vc = vn[c0:c0 + CHUNK]
        vstack = jnp.concatenate(
            [jnp.where(head_of_lane == h, vc, zero) for h in range(N_HEADS)], axis=0)
        outs.append(_dot(wcat, vstack) + sgub_ref[...])
    return u * jnp.concatenate(outs, axis=0)


def _short_conv(bg, cgh, dbuf, cdw_ref, ts):
    dbuf[pl.ds(D_HALO, ts), :] = cgh
    conv = cdw_ref[SHORT_CONV_WIDTH - 1:SHORT_CONV_WIDTH, :] * cgh
    for k in range(SHORT_CONV_WIDTH - 1):
        back = SHORT_CONV_WIDTH - 1 - k
        conv = conv + cdw_ref[k:k + 1, :] * dbuf[pl.ds(D_HALO - back, ts), :]
    dbuf[pl.ds(0, D_HALO), :] = dbuf[pl.ds(ts, D_HALO), :]
    return bg * conv


def _ffn_gated(h, wg_ref, wu_ref, fcw_ref, gcarry, mbuf, ts, col_blocks):
    for c0, c1 in col_blocks:
        cols = slice(c0, c1)
        g = _dot(h, wg_ref[0, :, cols])
        u = _dot(h, wu_ref[0, :, cols])
        ext = jnp.concatenate([gcarry[:, cols], g], axis=0)
        conv = fcw_ref[FFN_CONV_WIDTH - 1:FFN_CONV_WIDTH, cols] * g
        for k in range(FFN_CONV_WIDTH - 1):
            back = FFN_CONV_WIDTH - 1 - k
            conv = conv + fcw_ref[k:k + 1, cols] * pltpu.roll(ext, back, 0)[F_HALO:]
        gcarry[:, cols] = g[ts - F_HALO:]
        mbuf[:, cols] = (conv * _sigmoid(conv) * u).astype(BF16)


def _layer_kernel(x_ref, modc_ref, modp_ref, n1g_ref, w_in_ref, caw_ref, cab_ref, gng_ref, gnb_ref,
                  gmat_ref, poolw_ref, pools_ref, lng_ref, lnb_ref, sguw_ref, sgub_ref, cdw_ref,
                  w_out_ref, n2g_ref, wg_ref, wu_ref, fcw_ref, wd_ref, fing_ref, o_ref,
                  abuf, bbuf, dbuf, ybuf, hbuf, xmid, gcarry, mbuf,
                  *, ts, d_model, d_ff, tiles_per_seq, final_norm):
    step = pl.program_id(0)
    mix_first = lax.rem(step, tiles_per_seq) == 0
    ffn_first = (step == 0) | (lax.rem(step - 1, tiles_per_seq) == 0)

    @pl.when(step == 0)
    def _():
        hbuf[...] = jnp.zeros(hbuf.shape, BF16)
        xmid[...] = jnp.zeros(xmid.shape, F32)

    @pl.when(mix_first)
    def _():
        abuf[pl.ds(0, A_HALO), :] = jnp.zeros((A_HALO, W_GROUP), F32)
        bbuf[pl.ds(0, B_HALO), :] = jnp.zeros((B_HALO, W_GROUP), F32)
        dbuf[pl.ds(0, D_HALO), :] = jnp.zeros((D_HALO, W_GROUP), F32)

    @pl.when(ffn_first)
    def _():
        gcarry[...] = jnp.zeros(gcarry.shape, F32)

    modc = modc_ref[0]
    modp = modp_ref[0]

    def part(mod, i):
        return mod[:, i * d_model:(i + 1) * d_model]

    x_res = xmid[...]
    col_blocks = [(c0, min(c0 + FFN_COL_BLOCK, d_ff)) for c0 in range(0, d_ff, FFN_COL_BLOCK)]
    _ffn_gated(hbuf[...], wg_ref, wu_ref, fcw_ref, gcarry, mbuf, ts, col_blocks)

    x = x_ref[0]
    h1 = _modulated_rms_norm(x, n1g_ref[...], part(modc, 1), part(modc, 0)).astype(BF16)

    def proj(first, n):
        return _dot(h1, w_in_ref[0, :, first * W_GROUP:(first + n) * W_GROUP])

    def col(p, i):
        return p[:, i * W_GROUP:(i + 1) * W_GROUP]

    pa = proj(0, 2)
    conv = _causal_conv31(col(pa, 0) * _sigmoid(col(pa, 1)), abuf, caw_ref, cab_ref, ts)
    pb = proj(2, 1)
    pc = proj(3, 2)
    pd = proj(5, 3)

    f = _dot(mbuf[...], wd_ref[0])

    yb = _pool_mixer(pb, bbuf, poolw_ref, pools_ref, ts, mix_first)
    ybuf[:, 0:W_GROUP] = yb.astype(BF16)
    yc = _spatial_gating(col(pc, 0), col(pc, 1), lng_ref, lnb_ref, sguw_ref, sgub_ref, ts)
    ybuf[:, W_GROUP:2 * W_GROUP] = yc.astype(BF16)
    yd = _short_conv(col(pd, 0), col(pd, 1) * col(pd, 2), dbuf, cdw_ref, ts)
    ybuf[:, 2 * W_GROUP:3 * W_GROUP] = yd.astype(BF16)
    y = _dot(ybuf[...], w_out_ref[0, W_GROUP:4 * W_GROUP, :])
    ya = _group_norm_silu(conv, gng_ref, gnb_ref, gmat_ref[...])
    y = y + _dot(ya.astype(BF16), w_out_ref[0, 0:W_GROUP, :])
    x_mid = x + part(modc, 2) * y
    xmid[...] = x_mid
    hbuf[...] = _modulated_rms_norm(x_mid, n2g_ref[...], part(modc, 4), part(modc, 3)).astype(BF16)

    out = x_res + part(modp, 5) * f
    if final_norm:
        ms = jnp.mean(out * out, axis=-1, keepdims=True)
        out = out * lax.rsqrt(ms + EPS) * fing_ref[...]
    o_ref[0] = out


def _whole(shape):
    zeros = (0,) * len(shape)
    return pl.BlockSpec(shape, lambda *_: zeros, pipeline_mode=pl.Buffered(1))


def _layer(stacked, layer):
    index = (layer,) + (0,) * (stacked.ndim - 1)
    return pl.BlockSpec((1,) + stacked.shape[1:], lambda *_: index, pipeline_mode=pl.Buffered(1))


def _layer_call(x, mod_l, layer, final_norm, w_in, w_out, wg, wu, wd, n1g, caw, cab, gng, gnb, gmat,
                poolw, pools, lng, lnb, sguw, sgub, cdw, n2g, fcw, fing):
    batch, seq, d_model = x.shape
    d_ff = wg.shape[2]
    ts = SEQ_TILE
    tiles_per_seq = seq // ts
    n_tiles = batch * tiles_per_seq

    def cur(i):
        return jnp.minimum(i, n_tiles - 1)

    def prev(i):
        return jnp.maximum(i - 1, 0)

    def x_spec(tile):
        return pl.BlockSpec((1, ts, d_model),
                            lambda i: (tile(i) // tiles_per_seq, tile(i) % tiles_per_seq, 0))

    def mod_spec(tile):
        return pl.BlockSpec((1, 1, N_MOD * d_model), lambda i: (tile(i) // tiles_per_seq, 0, 0))

    mix_small = (caw, cab, gng, gnb, gmat, poolw, pools, lng, lnb, sguw, sgub, cdw)
    return pl.pallas_call(
        functools.partial(_layer_kernel, ts=ts, d_model=d_model, d_ff=d_ff,
                          tiles_per_seq=tiles_per_seq, final_norm=final_norm),
        grid=(n_tiles + 1,),
        in_specs=[x_spec(cur), mod_spec(cur), mod_spec(prev), _whole(n1g.shape), _layer(w_in, layer)]
        + [_whole(p.shape) for p in mix_small]
        + [_layer(w_out, layer), _whole(n2g.shape), _layer(wg, layer), _layer(wu, layer),
           _whole(fcw.shape), _layer(wd, layer), _whole(fing.shape)],
        out_specs=x_spec(prev),
        out_shape=jax.ShapeDtypeStruct(x.shape, x.dtype),
        scratch_shapes=[
            pltpu.VMEM((ts + A_HALO, W_GROUP), F32),
            pltpu.VMEM((ts + B_HALO, W_GROUP), F32),
            pltpu.VMEM((ts + D_HALO, W_GROUP), F32),
            pltpu.VMEM((ts, 3 * W_GROUP), BF16),
            pltpu.VMEM((ts, d_model), BF16),
            pltpu.VMEM((ts, d_model), F32),
            pltpu.VMEM((F_HALO, d_ff), F32),
            pltpu.VMEM((ts, d_ff), BF16),
        ],
        compiler_params=pltpu.CompilerParams(
            dimension_semantics=("arbitrary",),
            vmem_limit_bytes=VMEM_LIMIT_BYTES),
        name="layer",
    )(x, mod_l, mod_l, n1g, w_in, *mix_small, w_out, n2g, wg, wu, fcw, wd, fing)


def _block_diag(blocks):
    g, n, _ = blocks.shape
    eye = jnp.eye(g, dtype=blocks.dtype)
    return jnp.einsum("gij,gh->gihj", blocks, eye).reshape(g * n, g * n)


def kernel(x, c, norm1_g, ada_w, ada_b, w_in, conv_a_w, conv_a_b, gn_a_g, gn_a_b, pool_w, pool_scale, sgu_ln_g, sgu_ln_b, sgu_w, sgu_b, conv_d_w, w_out, norm2_g, ffn_w_gate, ffn_w_up, ffn_conv_w, ffn_w_down, final_g):
    batch, seq, d_model = x.shape
    n_layers = ada_w.shape[0]
    assert seq % SEQ_TILE == 0 and SEQ_TILE % CHUNK == 0 and SEQ_TILE % CONV_ROW_BLOCK == 0
    assert ada_w.shape[2] % MOD_COL_BLOCK == 0

    c_pad = jnp.pad(c, ((0, -batch % SUBLANES), (0, 0)))
    mod = _mod_call(c_pad, ada_w, ada_b)[:, :batch].reshape(n_layers, batch, 1, N_MOD * d_model)

    row = lambda v: v.reshape(1, -1)
    head = jnp.arange(W_GROUP) // HEAD_DIM
    gmat = (head[:, None] == head[None, :]).astype(BF16)
    w_in_b, w_out_b = w_in.astype(BF16), w_out.astype(BF16)
    wg_b, wu_b, wd_b = ffn_w_gate.astype(BF16), ffn_w_up.astype(BF16), ffn_w_down.astype(BF16)
    for l in range(n_layers):
        sguw = jnp.transpose(sgu_w[l], (1, 0, 2)).reshape(CHUNK, N_HEADS * CHUNK)
        sgub = jnp.repeat(jnp.transpose(sgu_b[l]), HEAD_DIM, axis=1)
        x = _layer_call(
            x, mod[l], l, l == n_layers - 1, w_in_b, w_out_b, wg_b, wu_b, wd_b, row(norm1_g[l]),
            conv_a_w[l], row(conv_a_b[l]), row(gn_a_g[l]), row(gn_a_b[l]), gmat,
            _block_diag(pool_w[l]).astype(BF16), row(pool_scale[l]), row(sgu_ln_g[l]),
            row(sgu_ln_b[l]), sguw, sgub, conv_d_w[l], row(norm2_g[l]), ffn_conv_w[l],
            row(final_g))
    return x
```

```python
import functools

import jax
import jax.numpy as jnp
from jax import lax
from jax.experimental import pallas as pl
from jax.experimental.pallas import tpu as pltpu

EPS = 1e-6
LOG2_E = 1.4426950408889634
N_MOD = 6
W_GROUP = 256
HEAD_DIM = 64
N_HEADS = W_GROUP // HEAD_DIM
CHUNK = 128
CONV_A_WIDTH = 31
POOL_WINDOWS = (2, 4, 8, 16)
SHORT_CONV_WIDTH = 3
FFN_CONV_WIDTH = 3

SUBLANES = 8
A_HALO = 32
B_HALO = 16
D_HALO = 8
F_HALO = 8
CONV_ROW_BLOCK = 64

SEQ_TILE = 512
MOD_COL_BLOCK = 2048
FFN_COL_BLOCK = 512
VMEM_LIMIT_BYTES = 56 * 1024 * 1024

F32 = jnp.float32
BF16 = jnp.bfloat16


def _dot(a, b):
    return jnp.dot(a, b, preferred_element_type=F32)


def _sigmoid(v):
    return 1.0 / (1.0 + jnp.exp2(v * (-LOG2_E)))


def _modulated_rms_norm(x, gain, scale, shift):
    ms = jnp.mean(x * x, axis=-1, keepdims=True)
    return (x * lax.rsqrt(ms + EPS)) * (gain * (1.0 + scale)) + shift


def _mod_kernel(c_ref, w_ref, b_ref, o_ref):
    c = c_ref[...]
    ca = c * _sigmoid(c)
    o_ref[0] = _dot(ca.astype(BF16), w_ref[0].astype(BF16)) + b_ref[0]


def _mod_call(c_pad, ada_w, ada_b):
    n_layers, d_model, n_out = ada_w.shape
    rows = c_pad.shape[0]
    nb = MOD_COL_BLOCK
    return pl.pallas_call(
        _mod_kernel,
        grid=(n_layers, n_out // nb),
        in_specs=[
            pl.BlockSpec((rows, d_model), lambda l, j: (0, 0)),
            pl.BlockSpec((1, d_model, nb), lambda l, j: (l, 0, j)),
            pl.BlockSpec((1, 1, nb), lambda l, j: (l, 0, j)),
        ],
        out_specs=pl.BlockSpec((1, rows, nb), lambda l, j: (l, 0, j)),
        out_shape=jax.ShapeDtypeStruct((n_layers, rows, n_out), F32),
        compiler_params=pltpu.CompilerParams(
            dimension_semantics=("arbitrary", "arbitrary"),
            vmem_limit_bytes=VMEM_LIMIT_BYTES),
        name="mod",
    )(c_pad, ada_w, ada_b.reshape(n_layers, 1, n_out))


def _causal_conv31(glu, abuf, caw_ref, cab_ref, ts):
    abuf[pl.ds(A_HALO, ts), :] = glu
    rb = CONV_ROW_BLOCK
    first_off = A_HALO - (CONV_A_WIDTH - 1)
    blocks = []
    for r0 in range(0, ts, rb):
        acc = None
        for b in range(SUBLANES):
            taps = [(o - first_off, o - b) for o in range(first_off, A_HALO + 1)
                    if o % SUBLANES == b]
            span = max(off for _, off in taps)
            win = abuf[pl.ds(r0 + b, rb + span), :]
            part = None
            for k, off in taps:
                term = caw_ref[k:k + 1, :] * win[off:off + rb]
                part = term if part is None else part + term
            acc = part if acc is None else acc + part
        blocks.append(acc)
    conv = jnp.concatenate(blocks, axis=0) + cab_ref[...]
    abuf[pl.ds(0, A_HALO), :] = abuf[pl.ds(ts, A_HALO), :]
    return conv


def _group_norm_silu(conv, gng_ref, gnb_ref, gmat):
    inv = 1.0 / HEAD_DIM
    conv_hi = conv.astype(BF16)
    conv_lo = (conv - conv_hi.astype(F32)).astype(BF16)
    mu = (_dot(conv_hi, gmat) + _dot(conv_lo, gmat)) * inv
    d = conv - mu
    var = _dot((d * d).astype(BF16), gmat) * inv
    y = d * lax.rsqrt(var + EPS) * gng_ref[...] + gnb_ref[...]
    return y * _sigmoid(y)


def _pool_mixer(pb, bbuf, poolw_ref, pools_ref, ts, is_first):
    bbuf[pl.ds(B_HALO, ts), :] = pb
    half = W_GROUP // 2
    ext = bbuf[pl.ds(0, ts + B_HALO), :]
    s2 = ext + pltpu.roll(ext, 1, 0)
    s4 = s2 + pltpu.roll(s2, 2, 0)
    s4_hi = s4[:, half:W_GROUP]
    s8 = s4_hi + pltpu.roll(s4_hi, 4, 0)
    s16 = s8[B_HALO:] + s8[B_HALO - 8:B_HALO - 8 + ts]
    bbuf[pl.ds(0, B_HALO), :] = bbuf[pl.ds(ts, B_HALO), :]

    low = lax.broadcasted_iota(jnp.int32, (1, half), 1) < (half // 2)
    sum_a = jnp.where(low, s2[B_HALO:, 0:half], s4[B_HALO:, 0:half])
    sum_b = jnp.where(low, s8[B_HALO:], s16)
    sums = jnp.concatenate([sum_a, sum_b], axis=-1)

    lane_full = lax.broadcasted_iota(jnp.int32, (1, W_GROUP), 1)
    win = jnp.where(lane_full < 64, 2.0,
                    jnp.where(lane_full < 128, 4.0, jnp.where(lane_full < 192, 8.0, 16.0)))
    pos = lax.broadcasted_iota(jnp.int32, (B_HALO, W_GROUP), 0).astype(F32) + 1.0
    count = jnp.where(is_first, jnp.minimum(pos, win), win)
    mean = jnp.concatenate([sums[0:B_HALO] / count, sums[B_HALO:] * (1.0 / win)], axis=0)
    y = mean - pb
    return _dot(y.astype(BF16), poolw_ref[...]) * pools_ref[...]


def _spatial_gating(u, v, lng_ref, lnb_ref, sguw_ref, sgub_ref, ts):
    mu = jnp.mean(v, axis=-1, keepdims=True)
    d = v - mu
    var = jnp.mean(d * d, axis=-1, keepdims=True)
    vn = (d * lax.rsqrt(var + EPS) * lng_ref[...] + lnb_ref[...]).astype(BF16)

    row = lax.broadcasted_iota(jnp.int32, (CHUNK, N_HEADS * CHUNK), 0)
    col = lax.broadcasted_iota(jnp.int32, (CHUNK, N_HEADS * CHUNK), 1)
    wcat = jnp.where((col & (CHUNK - 1)) <= row, sguw_ref[...], 0.0).astype(BF16)

    head_of_lane = lax.broadcasted_iota(jnp.int32, (CHUNK, W_GROUP), 1) // HEAD_DIM
    zero = jnp.zeros((CHUNK, W_GROUP), BF16)
    outs = []
    for c0 in range(0, ts, CHUNK):
        vc = vn[c0:c0 + CHUNK]
        vstack = jnp.concatenate(
            [jnp.where(head_of_lane == h, vc, zero) for h in range(N_HEADS)], axis=0)
        outs.append(_dot(wcat, vstack) + sgub_ref[...])
    return u * jnp.concatenate(outs, axis=0)


def _short_conv(bg, cgh, dbuf, cdw_ref, ts):
    dbuf[pl.ds(D_HALO, ts), :] = cgh
    conv = cdw_ref[SHORT_CONV_WIDTH - 1:SHORT_CONV_WIDTH, :] * cgh
    for k in range(SHORT_CONV_WIDTH - 1):
        back = SHORT_CONV_WIDTH - 1 - k
        conv = conv + cdw_ref[k:k + 1, :] * dbuf[pl.ds(D_HALO - back, ts), :]
    dbuf[pl.ds(0, D_HALO), :] = dbuf[pl.ds(ts, D_HALO), :]
    return bg * conv


def _ffn_gated(h, wg_ref, wu_ref, fcw_ref, gcarry, mbuf, ts, col_blocks):
    for c0, c1 in col_blocks:
        cols = slice(c0, c1)
        g = _dot(h, wg_ref[0, :, cols])
        u = _dot(h, wu_ref[0, :, cols])
        ext = jnp.concatenate([gcarry[:, cols], g], axis=0)
        conv = fcw_ref[FFN_CONV_WIDTH - 1:FFN_CONV_WIDTH, cols] * g
        for k in range(FFN_CONV_WIDTH - 1):
            back = FFN_CONV_WIDTH - 1 - k
            conv = conv + fcw_ref[k:k + 1, cols] * pltpu.roll(ext, back, 0)[F_HALO:]
        gcarry[:, cols] = g[ts - F_HALO:]
        mbuf[:, cols] = (conv * _sigmoid(conv) * u).astype(BF16)


def _layer_kernel(x_ref, modc_ref, modp_ref, n1g_ref, w_in_ref, caw_ref, cab_ref, gng_ref, gnb_ref,
                  gmat_ref, poolw_ref, pools_ref, lng_ref, lnb_ref, sguw_ref, sgub_ref, cdw_ref,
                  w_out_ref, n2g_ref, wg_ref, wu_ref, fcw_ref, wd_ref, fing_ref, o_ref,
                  abuf, bbuf, dbuf, ybuf, hbuf, xmid, gcarry, mbuf,
                  *, ts, d_model, d_ff, tiles_per_seq, n_tiles, final_norm):
    step = pl.program_id(0)
    mix_first = lax.rem(step, tiles_per_seq) == 0

    @pl.when(mix_first & (step < n_tiles))
    def _():
        abuf[pl.ds(0, A_HALO), :] = jnp.zeros((A_HALO, W_GROUP), F32)
        bbuf[pl.ds(0, B_HALO), :] = jnp.zeros((B_HALO, W_GROUP), F32)
        dbuf[pl.ds(0, D_HALO), :] = jnp.zeros((D_HALO, W_GROUP), F32)

    @pl.when(lax.rem(step - 1, tiles_per_seq) == 0)
    def _():
        gcarry[...] = jnp.zeros(gcarry.shape, F32)

    def part(mod, i):
        return mod[:, i * d_model:(i + 1) * d_model]

    def body(mix, ffn):
        if ffn:
            modp = modp_ref[0]
            x_res = xmid[...]
            col_blocks = [(c0, min(c0 + FFN_COL_BLOCK, d_ff))
                          for c0 in range(0, d_ff, FFN_COL_BLOCK)]
            _ffn_gated(hbuf[...], wg_ref, wu_ref, fcw_ref, gcarry, mbuf, ts, col_blocks)

        if mix:
            modc = modc_ref[0]
            x = x_ref[0]
            h1 = _modulated_rms_norm(x, n1g_ref[...], part(modc, 1), part(modc, 0)).astype(BF16)

            def proj(first, n):
                return _dot(h1, w_in_ref[0, :, first * W_GROUP:(first + n) * W_GROUP])

            def col(p, i):
                return p[:, i * W_GROUP:(i + 1) * W_GROUP]

            pa = proj(0, 2)
            conv = _causal_conv31(col(pa, 0) * _sigmoid(col(pa, 1)), abuf, caw_ref, cab_ref, ts)
            pb = proj(2, 1)
            pc = proj(3, 2)
            pd = proj(5, 3)

        if ffn:
            f = _dot(mbuf[...], wd_ref[0])

        if mix:
            yb = _pool_mixer(pb, bbuf, poolw_ref, pools_ref, ts, mix_first)
            ybuf[:, 0:W_GROUP] = yb.astype(BF16)
            yc = _spatial_gating(col(pc, 0), col(pc, 1), lng_ref, lnb_ref, sguw_ref, sgub_ref, ts)
            ybuf[:, W_GROUP:2 * W_GROUP] = yc.astype(BF16)
            yd = _short_conv(col(pd, 0), col(pd, 1) * col(pd, 2), dbuf, cdw_ref, ts)
            ybuf[:, 2 * W_GROUP:3 * W_GROUP] = yd.astype(BF16)
            y = _dot(ybuf[...], w_out_ref[0, W_GROUP:4 * W_GROUP, :])
            ya = _group_norm_silu(conv, gng_ref, gnb_ref, gmat_ref[...])
            y = y + _dot(ya.astype(BF16), w_out_ref[0, 0:W_GROUP, :])
            x_mid = x + part(modc, 2) * y
            xmid[...] = x_mid
            hbuf[...] = _modulated_rms_norm(
                x_mid, n2g_ref[...], part(modc, 4), part(modc, 3)).astype(BF16)

        if ffn:
            out = x_res + part(modp, 5) * f
            if final_norm:
                ms = jnp.mean(out * out, axis=-1, keepdims=True)
                out = out * lax.rsqrt(ms + EPS) * fing_ref[...]
            o_ref[0] = out

    pl.when(step == 0)(lambda: body(True, False))
    pl.when((step > 0) & (step < n_tiles))(lambda: body(True, True))
    pl.when(step == n_tiles)(lambda: body(False, True))


def _whole(shape):
    zeros = (0,) * len(shape)
    return pl.BlockSpec(shape, lambda *_: zeros, pipeline_mode=pl.Buffered(1))


def _layer(stacked, layer):
    index = (layer,) + (0,) * (stacked.ndim - 1)
    return pl.BlockSpec((1,) + stacked.shape[1:], lambda *_: index, pipeline_mode=pl.Buffered(1))


def _layer_call(x, mod_l, layer, final_norm, w_in, w_out, wg, wu, wd, n1g, caw, cab, gng, gnb, gmat,
                poolw, pools, lng, lnb, sguw, sgub, cdw, n2g, fcw, fing):
    batch, seq, d_model = x.shape
    d_ff = wg.shape[2]
    ts = SEQ_TILE
    tiles_per_seq = seq // ts
    n_tiles = batch * tiles_per_seq

    def cur(i):
        return jnp.minimum(i, n_tiles - 1)

    def prev(i):
        return jnp.maximum(i - 1, 0)

    def x_spec(tile):
        return pl.BlockSpec((1, ts, d_model),
                            lambda i: (tile(i) // tiles_per_seq, tile(i) % tiles_per_seq, 0))

    def mod_spec(tile):
        return pl.BlockSpec((1, 1, N_MOD * d_model), lambda i: (tile(i) // tiles_per_seq, 0, 0))

    mix_small = (caw, cab, gng, gnb, gmat, poolw, pools, lng, lnb, sguw, sgub, cdw)
    return pl.pallas_call(
        functools.partial(_layer_kernel, ts=ts, d_model=d_model, d_ff=d_ff,
                          tiles_per_seq=tiles_per_seq, n_tiles=n_tiles, final_norm=final_norm),
        grid=(n_tiles + 1,),
        in_specs=[x_spec(cur), mod_spec(cur), mod_spec(prev), _whole(n1g.shape), _layer(w_in, layer)]
        + [_whole(p.shape) for p in mix_small]
        + [_layer(w_out, layer), _whole(n2g.shape), _layer(wg, layer), _layer(wu, layer),
           _whole(fcw.shape), _layer(wd, layer), _whole(fing.shape)],
        out_specs=x_spec(prev),
        out_shape=jax.ShapeDtypeStruct(x.shape, x.dtype),
        scratch_shapes=[
            pltpu.VMEM((ts + A_HALO, W_GROUP), F32),
            pltpu.VMEM((ts + B_HALO, W_GROUP), F32),
            pltpu.VMEM((ts + D_HALO, W_GROUP), F32),
            pltpu.VMEM((ts, 3 * W_GROUP), BF16),
            pltpu.VMEM((ts, d_model), BF16),
            pltpu.VMEM((ts, d_model), F32),
            pltpu.VMEM((F_HALO, d_ff), F32),
            pltpu.VMEM((ts, d_ff), BF16),
        ],
        compiler_params=pltpu.CompilerParams(
            dimension_semantics=("arbitrary",),
            vmem_limit_bytes=VMEM_LIMIT_BYTES),
        name="layer",
    )(x, mod_l, mod_l, n1g, w_in, *mix_small, w_out, n2g, wg, wu, fcw, wd, fing)


def _block_diag(blocks):
    g, n, _ = blocks.shape
    eye = jnp.eye(g, dtype=blocks.dtype)
    return jnp.einsum("gij,gh->gihj", blocks, eye).reshape(g * n, g * n)


def kernel(x, c, norm1_g, ada_w, ada_b, w_in, conv_a_w, conv_a_b, gn_a_g, gn_a_b, pool_w, pool_scale, sgu_ln_g, sgu_ln_b, sgu_w, sgu_b, conv_d_w, w_out, norm2_g, ffn_w_gate, ffn_w_up, ffn_conv_w, ffn_w_down, final_g):
    batch, seq, d_model = x.shape
    n_layers = ada_w.shape[0]
    assert seq % SEQ_TILE == 0 and SEQ_TILE % CHUNK == 0 and SEQ_TILE % CONV_ROW_BLOCK == 0
    assert ada_w.shape[2] % MOD_COL_BLOCK == 0

    c_pad = jnp.pad(c, ((0, -batch % SUBLANES), (0, 0)))
    mod = _mod_call(c_pad, ada_w, ada_b)[:, :batch].reshape(n_layers, batch, 1, N_MOD * d_model)

    row = lambda v: v.reshape(1, -1)
    head = jnp.arange(W_GROUP) // HEAD_DIM
    gmat = (head[:, None] == head[None, :]).astype(BF16)
    w_in_b, w_out_b = w_in.astype(BF16), w_out.astype(BF16)
    wg_b, wu_b, wd_b = ffn_w_gate.astype(BF16), ffn_w_up.astype(BF16), ffn_w_down.astype(BF16)
    for l in range(n_layers):
        sguw = jnp.transpose(sgu_w[l], (1, 0, 2)).reshape(CHUNK, N_HEADS * CHUNK)
        sgub = jnp.repeat(jnp.transpose(sgu_b[l]), HEAD_DIM, axis=1)
        x = _layer_call(
            x, mod[l], l, l == n_layers - 1, w_in_b, w_out_b, wg_b, wu_b, wd_b, row(norm1_g[l]),
            conv_a_w[l], row(conv_a_b[l]), row(gn_a_g[l]), row(gn_a_b[l]), gmat,
            _block_diag(pool_w[l]).astype(BF16), row(pool_scale[l]), row(sgu_ln_g[l]),
            row(sgu_ln_b[l]), sguw, sgub, conv_d_w[l], row(norm2_g[l]), ffn_conv_w[l],
            row(final_g))
    return x
```

```python
import functools

import jax
import jax.numpy as jnp
from jax import lax
from jax.experimental import pallas as pl
from jax.experimental.pallas import tpu as pltpu

EPS = 1e-6
LOG2_E = 1.4426950408889634
N_MOD = 6
W_GROUP = 256
HEAD_DIM = 64
N_HEADS = W_GROUP // HEAD_DIM
CHUNK = 128
CONV_A_WIDTH = 31
POOL_WINDOWS = (2, 4, 8, 16)
SHORT_CONV_WIDTH = 3
FFN_CONV_WIDTH = 3

SUBLANES = 8
A_HALO = 32
B_HALO = 16
D_HALO = 8
F_HALO = 8
CONV_ROW_BLOCK = 64

SEQ_TILE = 512
MOD_COL_BLOCK = 2048
FFN_COL_BLOCK = 512
VMEM_LIMIT_BYTES = 56 * 1024 * 1024

F32 = jnp.float32
BF16 = jnp.bfloat16


def _dot(a, b):
    return jnp.dot(a, b, preferred_element_type=F32)


def _sigmoid(v):
    return 1.0 / (1.0 + jnp.exp2(v * (-LOG2_E)))


def _modulated_rms_norm(x, gain, scale, shift):
    ms = jnp.mean(x * x, axis=-1, keepdims=True)
    return (x * lax.rsqrt(ms + EPS)) * (gain * (1.0 + scale)) + shift


def _mod_kernel(c_ref, w_ref, b_ref, o_ref):
    c = c_ref[...]
    ca = c * _sigmoid(c)
    o_ref[0] = _dot(ca.astype(BF16), w_ref[0].astype(BF16)) + b_ref[0]


def _mod_call(c_pad, ada_w, ada_b):
    n_layers, d_model, n_out = ada_w.shape
    rows = c_pad.shape[0]
    nb = MOD_COL_BLOCK
    return pl.pallas_call(
        _mod_kernel,
        grid=(n_layers, n_out // nb),
        in_specs=[
            pl.BlockSpec((rows, d_model), lambda l, j: (0, 0)),
            pl.BlockSpec((1, d_model, nb), lambda l, j: (l, 0, j)),
            pl.BlockSpec((1, 1, nb), lambda l, j: (l, 0, j)),
        ],
        out_specs=pl.BlockSpec((1, rows, nb), lambda l, j: (l, 0, j)),
        out_shape=jax.ShapeDtypeStruct((n_layers, rows, n_out), F32),
        compiler_params=pltpu.CompilerParams(
            dimension_semantics=("arbitrary", "arbitrary"),
            vmem_limit_bytes=VMEM_LIMIT_BYTES),
        name="mod",
    )(c_pad, ada_w, ada_b.reshape(n_layers, 1, n_out))


def _causal_conv31(glu, abuf, caw_ref, cab_ref, ts):
    abuf[pl.ds(A_HALO, ts), :] = glu
    rb = CONV_ROW_BLOCK
    first_off = A_HALO - (CONV_A_WIDTH - 1)
    blocks = []
    for r0 in range(0, ts, rb):
        acc = None
        for b in range(SUBLANES):
            taps = [(o - first_off, o - b) for o in range(first_off, A_HALO + 1)
                    if o % SUBLANES == b]
            span = max(off for _, off in taps)
            win = abuf[pl.ds(r0 + b, rb + span), :]
            part = None
            for k, off in taps:
                term = caw_ref[k:k + 1, :] * win[off:off + rb]
                part = term if part is None else part + term
            acc = part if acc is None else acc + part
        blocks.append(acc)
    conv = jnp.concatenate(blocks, axis=0) + cab_ref[...]
    abuf[pl.ds(0, A_HALO), :] = abuf[pl.ds(ts, A_HALO), :]
    return conv


def _head_sums(v_bf16, gmat):
    return _dot(v_bf16, gmat)


def _head_mean(conv, gmat):
    conv_hi = conv.astype(BF16)
    conv_lo = (conv - conv_hi.astype(F32)).astype(BF16)
    return (_head_sums(conv_hi, gmat) + _head_sums(conv_lo, gmat)) * (1.0 / HEAD_DIM)


def _pool_mixer(pb, bbuf, poolw_ref, pools_ref, ts, is_first):
    bbuf[pl.ds(B_HALO, ts), :] = pb
    half = W_GROUP // 2
    ext = bbuf[pl.ds(0, ts + B_HALO), :]
    s2 = ext + pltpu.roll(ext, 1, 0)
    s4 = s2 + pltpu.roll(s2, 2, 0)
    s4_hi = s4[:, half:W_GROUP]
    s8 = s4_hi + pltpu.roll(s4_hi, 4, 0)
    s16 = s8[B_HALO:] + s8[B_HALO - 8:B_HALO - 8 + ts]
    bbuf[pl.ds(0, B_HALO), :] = bbuf[pl.ds(ts, B_HALO), :]

    low = lax.broadcasted_iota(jnp.int32, (1, half), 1) < (half // 2)
    sum_a = jnp.where(low, s2[B_HALO:, 0:half], s4[B_HALO:, 0:half])
    sum_b = jnp.where(low, s8[B_HALO:], s16)
    sums = jnp.concatenate([sum_a, sum_b], axis=-1)

    lane_full = lax.broadcasted_iota(jnp.int32, (1, W_GROUP), 1)
    win = jnp.where(lane_full < 64, 2.0,
                    jnp.where(lane_full < 128, 4.0, jnp.where(lane_full < 192, 8.0, 16.0)))
    pos = lax.broadcasted_iota(jnp.int32, (B_HALO, W_GROUP), 0).astype(F32) + 1.0
    count = jnp.where(is_first, jnp.minimum(pos, win), win)
    mean = jnp.concatenate([sums[0:B_HALO] / count, sums[B_HALO:] * (1.0 / win)], axis=0)
    y = mean - pb
    return _dot(y.astype(BF16), poolw_ref[...]) * pools_ref[...]


def _spatial_gating(u, v, lng_ref, lnb_ref, sguw_ref, sgub_ref, ts):
    mu = jnp.mean(v, axis=-1, keepdims=True)
    d = v - mu
    var = jnp.mean(d * d, axis=-1, keepdims=True)
    vn = (d * lax.rsqrt(var + EPS) * lng_ref[...] + lnb_ref[...]).astype(BF16)

    row = lax.broadcasted_iota(jnp.int32, (CHUNK, N_HEADS * CHUNK), 0)
    col = lax.broadcasted_iota(jnp.int32, (CHUNK, N_HEADS * CHUNK), 1)
    wcat = jnp.where((col & (CHUNK - 1)) <= row, sguw_ref[...], 0.0).astype(BF16)

    head_of_lane = lax.broadcasted_iota(jnp.int32, (CHUNK, W_GROUP), 1) // HEAD_DIM
    zero = jnp.zeros((CHUNK, W_GROUP), BF16)
    outs = []
    for c0 in range(0, ts, CHUNK):
        vc = vn[c0:c0 + CHUNK]
        vstack = jnp.concatenate(
            [jnp.where(head_of_lane == h, vc, zero) for h in range(N_HEADS)], axis=0)
        outs.append(_dot(wcat, vstack) + sgub_ref[...])
    return u * jnp.concatenate(outs, axis=0)


def _short_conv(bg, cgh, dbuf, cdw_ref, ts):
    dbuf[pl.ds(D_HALO, ts), :] = cgh
    conv = cdw_ref[SHORT_CONV_WIDTH - 1:SHORT_CONV_WIDTH, :] * cgh
    for k in range(SHORT_CONV_WIDTH - 1):
        back = SHORT_CONV_WIDTH - 1 - k
        conv = conv + cdw_ref[k:k + 1, :] * dbuf[pl.ds(D_HALO - back, ts), :]
    dbuf[pl.ds(0, D_HALO), :] = dbuf[pl.ds(ts, D_HALO), :]
    return bg * conv


def _ffn_gated(h, wg_ref, wu_ref, fcw_ref, gcarry, mbuf, ts, col_blocks):
    for c0, c1 in col_blocks:
        cols = slice(c0, c1)
        g = _dot(h, wg_ref[0, :, cols])
        u = _dot(h, wu_ref[0, :, cols])
        ext = jnp.concatenate([gcarry[:, cols], g], axis=0)
        conv = fcw_ref[FFN_CONV_WIDTH - 1:FFN_CONV_WIDTH, cols] * g
        for k in range(FFN_CONV_WIDTH - 1):
            back = FFN_CONV_WIDTH - 1 - k
            conv = conv + fcw_ref[k:k + 1, cols] * pltpu.roll(ext, back, 0)[F_HALO:]
        gcarry[:, cols] = g[ts - F_HALO:]
        mbuf[:, cols] = (conv * _sigmoid(conv) * u).astype(BF16)


def _layer_kernel(x_ref, modc_ref, modp_ref, n1g_ref, w_in_ref, caw_ref, cab_ref, gng_ref, gnb_ref,
                  gmat_ref, poolw_ref, pools_ref, lng_ref, lnb_ref, sguw_ref, sgub_ref, cdw_ref,
                  w_out_ref, n2g_ref, wg_ref, wu_ref, fcw_ref, wd_ref, fing_ref, o_ref,
                  abuf, bbuf, dbuf, ybuf, hbuf, xmid, gcarry, mbuf,
                  *, ts, d_model, d_ff, tiles_per_seq, final_norm):
    step = pl.program_id(0)
    mix_first = lax.rem(step, tiles_per_seq) == 0
    ffn_first = (step == 0) | (lax.rem(step - 1, tiles_per_seq) == 0)

    @pl.when(step == 0)
    def _():
        hbuf[...] = jnp.zeros(hbuf.shape, BF16)
        xmid[...] = jnp.zeros(xmid.shape, F32)

    @pl.when(mix_first)
    def _():
        abuf[pl.ds(0, A_HALO), :] = jnp.zeros((A_HALO, W_GROUP), F32)
        bbuf[pl.ds(0, B_HALO), :] = jnp.zeros((B_HALO, W_GROUP), F32)
        dbuf[pl.ds(0, D_HALO), :] = jnp.zeros((D_HALO, W_GROUP), F32)

    @pl.when(ffn_first)
    def _():
        gcarry[...] = jnp.zeros(gcarry.shape, F32)

    modc = modc_ref[0]
    modp = modp_ref[0]

    def part(mod, i):
        return mod[:, i * d_model:(i + 1) * d_model]

    x_res = xmid[...]
    col_blocks = [(c0, min(c0 + FFN_COL_BLOCK, d_ff)) for c0 in range(0, d_ff, FFN_COL_BLOCK)]
    _ffn_gated(hbuf[...], wg_ref, wu_ref, fcw_ref, gcarry, mbuf, ts, col_blocks)

    x = x_ref[0]
    h1 = _modulated_rms_norm(x, n1g_ref[...], part(modc, 1), part(modc, 0)).astype(BF16)

    def proj(first, n):
        return _dot(h1, w_in_ref[0, :, first * W_GROUP:(first + n) * W_GROUP])

    def col(p, i):
        return p[:, i * W_GROUP:(i + 1) * W_GROUP]

    pa = proj(0, 2)
    conv = _causal_conv31(col(pa, 0) * _sigmoid(col(pa, 1)), abuf, caw_ref, cab_ref, ts)
    pb = proj(2, 1)
    pc = proj(3, 2)
    pd = proj(5, 3)

    def down(c0, c1):
        return _dot(mbuf[...], wd_ref[0, :, c0:c1])

    f_parts = [down(0, 2 * W_GROUP)]

    yb = _pool_mixer(pb, bbuf, poolw_ref, pools_ref, ts, mix_first)
    ybuf[:, 0:W_GROUP] = yb.astype(BF16)
    yc = _spatial_gating(col(pc, 0), col(pc, 1), lng_ref, lnb_ref, sguw_ref, sgub_ref, ts)
    ybuf[:, W_GROUP:2 * W_GROUP] = yc.astype(BF16)
    yd = _short_conv(col(pd, 0), col(pd, 1) * col(pd, 2), dbuf, cdw_ref, ts)
    ybuf[:, 2 * W_GROUP:3 * W_GROUP] = yd.astype(BF16)
    y = _dot(ybuf[...], w_out_ref[0, W_GROUP:4 * W_GROUP, :])
    gmat = gmat_ref[...]
    dev = conv - _head_mean(conv, gmat)
    f_parts.append(down(2 * W_GROUP, 3 * W_GROUP))
    var = _head_sums((dev * dev).astype(BF16), gmat) * (1.0 / HEAD_DIM)
    ya = dev * lax.rsqrt(var + EPS) * gng_ref[...] + gnb_ref[...]
    y = y + _dot((ya * _sigmoid(ya)).astype(BF16), w_out_ref[0, 0:W_GROUP, :])
    x_mid = x + part(modc, 2) * y
    xmid[...] = x_mid
    hbuf[...] = _modulated_rms_norm(x_mid, n2g_ref[...], part(modc, 4), part(modc, 3)).astype(BF16)
    f_parts.append(down(3 * W_GROUP, 4 * W_GROUP))
    f = jnp.concatenate(f_parts, axis=-1)

    out = x_res + part(modp, 5) * f
    if final_norm:
        ms = jnp.mean(out * out, axis=-1, keepdims=True)
        out = out * lax.rsqrt(ms + EPS) * fing_ref[...]
    o_ref[0] = out


def _whole(shape):
    zeros = (0,) * len(shape)
    return pl.BlockSpec(shape, lambda *_: zeros, pipeline_mode=pl.Buffered(1))


def _layer(stacked, layer):
    index = (layer,) + (0,) * (stacked.ndim - 1)
    return pl.BlockSpec((1,) + stacked.shape[1:], lambda *_: index, pipeline_mode=pl.Buffered(1))


def _layer_call(x, mod_l, layer, final_norm, w_in, w_out, wg, wu, wd, n1g, caw, cab, gng, gnb, gmat,
                poolw, pools, lng, lnb, sguw, sgub, cdw, n2g, fcw, fing):
    batch, seq, d_model = x.shape
    d_ff = wg.shape[2]
    ts = SEQ_TILE
    tiles_per_seq = seq // ts
    n_tiles = batch * tiles_per_seq

    def cur(i):
        return jnp.minimum(i, n_tiles - 1)

    def prev(i):
        return jnp.maximum(i - 1, 0)

    def x_spec(tile):
        return pl.BlockSpec((1, ts, d_model),
                            lambda i: (tile(i) // tiles_per_seq, tile(i) % tiles_per_seq, 0))

    def mod_spec(tile):
        return pl.BlockSpec((1, 1, N_MOD * d_model), lambda i: (tile(i) // tiles_per_seq, 0, 0))

    mix_small = (caw, cab, gng, gnb, gmat, poolw, pools, lng, lnb, sguw, sgub, cdw)
    return pl.pallas_call(
        functools.partial(_layer_kernel, ts=ts, d_model=d_model, d_ff=d_ff,
                          tiles_per_seq=tiles_per_seq, final_norm=final_norm),
        grid=(n_tiles + 1,),
        in_specs=[x_spec(cur), mod_spec(cur), mod_spec(prev), _whole(n1g.shape), _layer(w_in, layer)]
        + [_whole(p.shape) for p in mix_small]
        + [_layer(w_out, layer), _whole(n2g.shape), _layer(wg, layer), _layer(wu, layer),
           _whole(fcw.shape), _layer(wd, layer), _whole(fing.shape)],
        out_specs=x_spec(prev),
        out_shape=jax.ShapeDtypeStruct(x.shape, x.dtype),
        scratch_shapes=[
            pltpu.VMEM((ts + A_HALO, W_GROUP), F32),
            pltpu.VMEM((ts + B_HALO, W_GROUP), F32),
            pltpu.VMEM((ts + D_HALO, W_GROUP), F32),
            pltpu.VMEM((ts, 3 * W_GROUP), BF16),
            pltpu.VMEM((ts, d_model), BF16),
            pltpu.VMEM((ts, d_model), F32),
            pltpu.VMEM((F_HALO, d_ff), F32),
            pltpu.VMEM((ts, d_ff), BF16),
        ],
        compiler_params=pltpu.CompilerParams(
            dimension_semantics=("arbitrary",),
            vmem_limit_bytes=VMEM_LIMIT_BYTES),
        name="layer",
    )(x, mod_l, mod_l, n1g, w_in, *mix_small, w_out, n2g, wg, wu, fcw, wd, fing)


def _block_diag(blocks):
    g, n, _ = blocks.shape
    eye = jnp.eye(g, dtype=blocks.dtype)
    return jnp.einsum("gij,gh->gihj", blocks, eye).reshape(g * n, g * n)


def kernel(x, c, norm1_g, ada_w, ada_b, w_in, conv_a_w, conv_a_b, gn_a_g, gn_a_b, pool_w, pool_scale, sgu_ln_g, sgu_ln_b, sgu_w, sgu_b, conv_d_w, w_out, norm2_g, ffn_w_gate, ffn_w_up, ffn_conv_w, ffn_w_down, final_g):
    batch, seq, d_model = x.shape
    n_layers = ada_w.shape[0]
    assert seq % SEQ_TILE == 0 and SEQ_TILE % CHUNK == 0 and SEQ_TILE % CONV_ROW_BLOCK == 0
    assert ada_w.shape[2] % MOD_COL_BLOCK == 0

    c_pad = jnp.pad(c, ((0, -batch % SUBLANES), (0, 0)))
    mod = _mod_call(c_pad, ada_w, ada_b)[:, :batch].reshape(n_layers, batch, 1, N_MOD * d_model)

    row = lambda v: v.reshape(1, -1)
    head = jnp.arange(W_GROUP) // HEAD_DIM
    gmat = (head[:, None] == head[None, :]).astype(BF16)
    w_in_b, w_out_b = w_in.astype(BF16), w_out.astype(BF16)
    wg_b, wu_b, wd_b = ffn_w_gate.astype(BF16), ffn_w_up.astype(BF16), ffn_w_down.astype(BF16)
    for l in range(n_layers):
        sguw = jnp.transpose(sgu_w[l], (1, 0, 2)).reshape(CHUNK, N_HEADS * CHUNK)
        sgub = jnp.repeat(jnp.transpose(sgu_b[l]), HEAD_DIM, axis=1)
        x = _layer_call(
            x, mod[l], l, l == n_layers - 1, w_in_b, w_out_b, wg_b, wu_b, wd_b, row(norm1_g[l]),
            conv_a_w[l], row(conv_a_b[l]), row(gn_a_g[l]), row(gn_a_b[l]), gmat,
            _block_diag(pool_w[l]).astype(BF16), row(pool_scale[l]), row(sgu_ln_g[l]),
            row(sgu_ln_b[l]), sguw, sgub, conv_d_w[l], row(norm2_g[l]), ffn_conv_w[l],
            row(final_g))
    return x
```

```python
import functools

import jax
import jax.numpy as jnp
from jax import lax
from jax.experimental import pallas as pl
from jax.experimental.pallas import tpu as pltpu

EPS = 1e-6
LOG2_E = 1.4426950408889634
N_MOD = 6
W_GROUP = 256
HEAD_DIM = 64
N_HEADS = W_GROUP // HEAD_DIM
CHUNK = 128
CONV_A_WIDTH = 31
POOL_WINDOWS = (2, 4, 8, 16)
SHORT_CONV_WIDTH = 3
FFN_CONV_WIDTH = 3

SUBLANES = 8
A_HALO = 32
B_HALO = 16
D_HALO = 8
F_HALO = 8
CONV_ROW_BLOCK = 64

SEQ_TILE = 512
MOD_COL_BLOCK = 2048
FFN_COL_BLOCK = 512
N_LAYER_INPUTS = 24
BF16_ROWS = 16
VMEM_LIMIT_BYTES = 56 * 1024 * 1024

F32 = jnp.float32
BF16 = jnp.bfloat16


def _dot(a, b):
    return jnp.dot(a, b, preferred_element_type=F32)


def _sigmoid(v):
    return 1.0 / (1.0 + jnp.exp2(v * (-LOG2_E)))


def _modulated_rms_norm(x, gain, scale, shift):
    ms = jnp.mean(x * x, axis=-1, keepdims=True)
    return (x * lax.rsqrt(ms + EPS)) * (gain * (1.0 + scale)) + shift


def _mod_kernel(c_ref, w_ref, b_ref, o_ref):
    c = c_ref[...]
    ca = c * _sigmoid(c)
    o_ref[0] = _dot(ca.astype(BF16), w_ref[0].astype(BF16)) + b_ref[0]


def _mod_call(c_pad, ada_w, ada_b):
    n_layers, d_model, n_out = ada_w.shape
    rows = c_pad.shape[0]
    nb = MOD_COL_BLOCK
    return pl.pallas_call(
        _mod_kernel,
        grid=(n_layers, n_out // nb),
        in_specs=[
            pl.BlockSpec((rows, d_model), lambda l, j: (0, 0)),
            pl.BlockSpec((1, d_model, nb), lambda l, j: (l, 0, j)),
            pl.BlockSpec((1, 1, nb), lambda l, j: (l, 0, j)),
        ],
        out_specs=pl.BlockSpec((1, rows, nb), lambda l, j: (l, 0, j)),
        out_shape=jax.ShapeDtypeStruct((n_layers, rows, n_out), F32),
        compiler_params=pltpu.CompilerParams(
            dimension_semantics=("arbitrary", "arbitrary"),
            vmem_limit_bytes=VMEM_LIMIT_BYTES),
        name="mod",
    )(c_pad, ada_w, ada_b.reshape(n_layers, 1, n_out))


def _causal_conv31(glu, abuf, caw_ref, cab_ref, ts):
    abuf[pl.ds(A_HALO, ts), :] = glu
    rb = CONV_ROW_BLOCK
    first_off = A_HALO - (CONV_A_WIDTH - 1)
    blocks = []
    for r0 in range(0, ts, rb):
        acc = None
        for b in range(SUBLANES):
            taps = [(o - first_off, o - b) for o in range(first_off, A_HALO + 1)
                    if o % SUBLANES == b]
            span = max(off for _, off in taps)
            win = abuf[pl.ds(r0 + b, rb + span), :]
            part = None
            for k, off in taps:
                term = caw_ref[k:k + 1, :] * win[off:off + rb]
                part = term if part is None else part + term
            acc = part if acc is None else acc + part
        blocks.append(acc)
    conv = jnp.concatenate(blocks, axis=0) + cab_ref[...]
    abuf[pl.ds(0, A_HALO), :] = abuf[pl.ds(ts, A_HALO), :]
    return conv


def _group_norm_silu(conv, gng_ref, gnb_ref, gmat):
    inv = 1.0 / HEAD_DIM
    conv_hi = conv.astype(BF16)
    conv_lo = (conv - conv_hi.astype(F32)).astype(BF16)
    mu = (_dot(conv_hi, gmat) + _dot(conv_lo, gmat)) * inv
    d = conv - mu
    var = _dot((d * d).astype(BF16), gmat) * inv
    y = d * lax.rsqrt(var + EPS) * gng_ref[...] + gnb_ref[...]
    return y * _sigmoid(y)


def _pool_mixer(pb, bbuf, poolw_ref, pools_ref, ts, is_first):
    bbuf[pl.ds(B_HALO, ts), :] = pb
    half = W_GROUP // 2
    ext = bbuf[pl.ds(0, ts + B_HALO), :]
    s2 = ext + pltpu.roll(ext, 1, 0)
    s4 = s2 + pltpu.roll(s2, 2, 0)
    s4_hi = s4[:, half:W_GROUP]
    s8 = s4_hi + pltpu.roll(s4_hi, 4, 0)
    s16 = s8[B_HALO:] + s8[B_HALO - 8:B_HALO - 8 + ts]
    bbuf[pl.ds(0, B_HALO), :] = bbuf[pl.ds(ts, B_HALO), :]

    low = lax.broadcasted_iota(jnp.int32, (1, half), 1) < (half // 2)
    sum_a = jnp.where(low, s2[B_HALO:, 0:half], s4[B_HALO:, 0:half])
    sum_b = jnp.where(low, s8[B_HALO:], s16)
    sums = jnp.concatenate([sum_a, sum_b], axis=-1)

    lane_full = lax.broadcasted_iota(jnp.int32, (1, W_GROUP), 1)
    win = jnp.where(lane_full < 64, 2.0,
                    jnp.where(lane_full < 128, 4.0, jnp.where(lane_full < 192, 8.0, 16.0)))
    pos = lax.broadcasted_iota(jnp.int32, (B_HALO, W_GROUP), 0).astype(F32) + 1.0
    count = jnp.where(is_first, jnp.minimum(pos, win), win)
    mean = jnp.concatenate([sums[0:B_HALO] / count, sums[B_HALO:] * (1.0 / win)], axis=0)
    y = mean - pb
    return _dot(y.astype(BF16), poolw_ref[...]) * pools_ref[...]


def _spatial_gating(u, v, lng_ref, lnb_ref, sguw_ref, sgub_ref, ts):
    mu = jnp.mean(v, axis=-1, keepdims=True)
    d = v - mu
    var = jnp.mean(d * d, axis=-1, keepdims=True)
    vn = (d * lax.rsqrt(var + EPS) * lng_ref[...] + lnb_ref[...]).astype(BF16)

    row = lax.broadcasted_iota(jnp.int32, (CHUNK, N_HEADS * CHUNK), 0)
    col = lax.broadcasted_iota(jnp.int32, (CHUNK, N_HEADS * CHUNK), 1)
    wcat = jnp.where((col & (CHUNK - 1)) <= row, sguw_ref[...], 0.0).astype(BF16)

    head_of_lane = lax.broadcasted_iota(jnp.int32, (CHUNK, W_GROUP), 1) // HEAD_DIM
    zero = jnp.zeros((CHUNK, W_GROUP), BF16)
    outs = []
    for c0 in range(0, ts, CHUNK):
        vc = vn[c0:c0 + CHUNK]
        vstack = jnp.concatenate(
            [jnp.where(head_of_lane == h, vc, zero) for h in range(N_HEADS)], axis=0)
        outs.append(_dot(wcat, vstack) + sgub_ref[...])
    return u * jnp.concatenate(outs, axis=0)


def _short_conv(bg, cgh, dbuf, cdw_ref, ts):
    dbuf[pl.ds(D_HALO, ts), :] = cgh
    conv = cdw_ref[SHORT_CONV_WIDTH - 1:SHORT_CONV_WIDTH, :] * cgh
    for k in range(SHORT_CONV_WIDTH - 1):
        back = SHORT_CONV_WIDTH - 1 - k
        conv = conv + cdw_ref[k:k + 1, :] * dbuf[pl.ds(D_HALO - back, ts), :]
    dbuf[pl.ds(0, D_HALO), :] = dbuf[pl.ds(ts, D_HALO), :]
    return bg * conv


def _ffn_gated(h, wg_ref, wu_ref, fcw_ref, gcarry, mbuf, ts, col_blocks):
    for c0, c1 in col_blocks:
        cols = slice(c0, c1)
        g = _dot(h, wg_ref[0, :, cols])
        u = _dot(h, wu_ref[0, :, cols])
        ext = jnp.concatenate([gcarry[:, cols], g], axis=0)
        conv = fcw_ref[FFN_CONV_WIDTH - 1:FFN_CONV_WIDTH, cols] * g
        for k in range(FFN_CONV_WIDTH - 1):
            back = FFN_CONV_WIDTH - 1 - k
            conv = conv + fcw_ref[k:k + 1, cols] * pltpu.roll(ext, back, 0)[F_HALO:]
        gcarry[:, cols] = g[ts - F_HALO:]
        mbuf[:, cols] = (conv * _sigmoid(conv) * u).astype(BF16)


def _layer_kernel(*refs, ts, d_model, d_ff, tiles_per_seq, final_norm, n_cast):
    (x_ref, modc_ref, modp_ref, n1g_ref, w_in_ref, caw_ref, cab_ref, gng_ref, gnb_ref, gmat_ref,
     poolw_ref, pools_ref, lng_ref, lnb_ref, sguw_ref, sgub_ref, cdw_ref, w_out_ref, n2g_ref,
     wg_ref, wu_ref, fcw_ref, wd_ref, fing_ref) = refs[:N_LAYER_INPUTS]
    cast_src = refs[N_LAYER_INPUTS:N_LAYER_INPUTS + n_cast]
    o_ref = refs[N_LAYER_INPUTS + n_cast]
    cast_dst = refs[N_LAYER_INPUTS + n_cast + 1:N_LAYER_INPUTS + 2 * n_cast + 1]
    abuf, bbuf, dbuf, ybuf, hbuf, xmid, gcarry, mbuf = refs[N_LAYER_INPUTS + 2 * n_cast + 1:]

    for src_ref, dst_ref in zip(cast_src, cast_dst):
        dst_ref[...] = src_ref[...].astype(BF16)

    step = pl.program_id(0)
    mix_first = lax.rem(step, tiles_per_seq) == 0
    ffn_first = (step == 0) | (lax.rem(step - 1, tiles_per_seq) == 0)

    @pl.when(step == 0)
    def _():
        hbuf[...] = jnp.zeros(hbuf.shape, BF16)
        xmid[...] = jnp.zeros(xmid.shape, F32)

    @pl.when(mix_first)
    def _():
        abuf[pl.ds(0, A_HALO), :] = jnp.zeros((A_HALO, W_GROUP), F32)
        bbuf[pl.ds(0, B_HALO), :] = jnp.zeros((B_HALO, W_GROUP), F32)
        dbuf[pl.ds(0, D_HALO), :] = jnp.zeros((D_HALO, W_GROUP), F32)

    @pl.when(ffn_first)
    def _():
        gcarry[...] = jnp.zeros(gcarry.shape, F32)

    modc = modc_ref[0]
    modp = modp_ref[0]

    def part(mod, i):
        return mod[:, i * d_model:(i + 1) * d_model]

    x_res = xmid[...]
    col_blocks = [(c0, min(c0 + FFN_COL_BLOCK, d_ff)) for c0 in range(0, d_ff, FFN_COL_BLOCK)]
    _ffn_gated(hbuf[...], wg_ref, wu_ref, fcw_ref, gcarry, mbuf, ts, col_blocks)

    x = x_ref[0]
    h1 = _modulated_rms_norm(x, n1g_ref[...], part(modc, 1), part(modc, 0)).astype(BF16)

    def proj(first, n):
        return _dot(h1, w_in_ref[0, :, first * W_GROUP:(first + n) * W_GROUP])

    def col(p, i):
        return p[:, i * W_GROUP:(i + 1) * W_GROUP]

    pa = proj(0, 2)
    conv = _causal_conv31(col(pa, 0) * _sigmoid(col(pa, 1)), abuf, caw_ref, cab_ref, ts)
    pb = proj(2, 1)
    pc = proj(3, 2)
    pd = proj(5, 3)

    f = _dot(mbuf[...], wd_ref[0])

    yb = _pool_mixer(pb, bbuf, poolw_ref, pools_ref, ts, mix_first)
    ybuf[:, 0:W_GROUP] = yb.astype(BF16)
    yc = _spatial_gating(col(pc, 0), col(pc, 1), lng_ref, lnb_ref, sguw_ref, sgub_ref, ts)
    ybuf[:, W_GROUP:2 * W_GROUP] = yc.astype(BF16)
    yd = _short_conv(col(pd, 0), col(pd, 1) * col(pd, 2), dbuf, cdw_ref, ts)
    ybuf[:, 2 * W_GROUP:3 * W_GROUP] = yd.astype(BF16)
    y = _dot(ybuf[...], w_out_ref[0, W_GROUP:4 * W_GROUP, :])
    ya = _group_norm_silu(conv, gng_ref, gnb_ref, gmat_ref[...])
    y = y + _dot(ya.astype(BF16), w_out_ref[0, 0:W_GROUP, :])
    x_mid = x + part(modc, 2) * y
    xmid[...] = x_mid
    hbuf[...] = _modulated_rms_norm(x_mid, n2g_ref[...], part(modc, 4), part(modc, 3)).astype(BF16)

    out = x_res + part(modp, 5) * f
    if final_norm:
        ms = jnp.mean(out * out, axis=-1, keepdims=True)
        out = out * lax.rsqrt(ms + EPS) * fing_ref[...]
    o_ref[0] = out


def _whole(shape):
    zeros = (0,) * len(shape)
    return pl.BlockSpec(shape, lambda *_: zeros, pipeline_mode=pl.Buffered(1))


def _layer(stacked, layer):
    index = (layer,) + (0,) * (stacked.ndim - 1)
    return pl.BlockSpec((1,) + stacked.shape[1:], lambda *_: index, pipeline_mode=pl.Buffered(1))


def _cast_row_block(rows, n_steps):
    block = BF16_ROWS
    while rows % block or rows // block > n_steps:
        block += BF16_ROWS
    return block


def _layer_call(x, mod_l, layer, final_norm, w_in, w_out, wg, wu, wd, n1g, caw, cab, gng, gnb, gmat,
                poolw, pools, lng, lnb, sguw, sgub, cdw, n2g, fcw, fing, cast_next, next_layer):
    batch, seq, d_model = x.shape
    d_ff = wg.shape[2]
    ts = SEQ_TILE
    tiles_per_seq = seq // ts
    n_tiles = batch * tiles_per_seq

    def cur(i):
        return jnp.minimum(i, n_tiles - 1)

    def prev(i):
        return jnp.maximum(i - 1, 0)

    def x_spec(tile):
        return pl.BlockSpec((1, ts, d_model),
                            lambda i: (tile(i) // tiles_per_seq, tile(i) % tiles_per_seq, 0))

    def mod_spec(tile):
        return pl.BlockSpec((1, 1, N_MOD * d_model), lambda i: (tile(i) // tiles_per_seq, 0, 0))

    def cast_spec(w, which):
        rows, cols = w.shape[1:]
        block = _cast_row_block(rows, n_tiles)
        return pl.BlockSpec((1, block, cols),
                            lambda i: (which, jnp.minimum(i, rows // block - 1), 0))

    mix_small = (caw, cab, gng, gnb, gmat, poolw, pools, lng, lnb, sguw, sgub, cdw)
    outs = pl.pallas_call(
        functools.partial(_layer_kernel, ts=ts, d_model=d_model, d_ff=d_ff,
                          tiles_per_seq=tiles_per_seq, final_norm=final_norm,
                          n_cast=len(cast_next)),
        grid=(n_tiles + 1,),
        in_specs=[x_spec(cur), mod_spec(cur), mod_spec(prev), _whole(n1g.shape), _layer(w_in, layer)]
        + [_whole(p.shape) for p in mix_small]
        + [_layer(w_out, layer), _whole(n2g.shape), _layer(wg, layer), _layer(wu, layer),
           _whole(fcw.shape), _layer(wd, layer), _whole(fing.shape)]
        + [cast_spec(w, next_layer) for w in cast_next],
        out_specs=[x_spec(prev)] + [cast_spec(w, 0) for w in cast_next],
        out_shape=[jax.ShapeDtypeStruct(x.shape, x.dtype)]
        + [jax.ShapeDtypeStruct((1,) + w.shape[1:], BF16) for w in cast_next],
        scratch_shapes=[
            pltpu.VMEM((ts + A_HALO, W_GROUP), F32),
            pltpu.VMEM((ts + B_HALO, W_GROUP), F32),
            pltpu.VMEM((ts + D_HALO, W_GROUP), F32),
            pltpu.VMEM((ts, 3 * W_GROUP), BF16),
            pltpu.VMEM((ts, d_model), BF16),
            pltpu.VMEM((ts, d_model), F32),
            pltpu.VMEM((F_HALO, d_ff), F32),
            pltpu.VMEM((ts, d_ff), BF16),
        ],
        compiler_params=pltpu.CompilerParams(
            dimension_semantics=("arbitrary",),
            vmem_limit_bytes=VMEM_LIMIT_BYTES),
        name="layer",
    )(x, mod_l, mod_l, n1g, w_in, *mix_small, w_out, n2g, wg, wu, fcw, wd, fing, *cast_next)
    return outs[0], outs[1:]


def _block_diag(blocks):
    g, n, _ = blocks.shape
    eye = jnp.eye(g, dtype=blocks.dtype)
    return jnp.einsum("gij,gh->gihj", blocks, eye).reshape(g * n, g * n)


def kernel(x, c, norm1_g, ada_w, ada_b, w_in, conv_a_w, conv_a_b, gn_a_g, gn_a_b, pool_w, pool_scale, sgu_ln_g, sgu_ln_b, sgu_w, sgu_b, conv_d_w, w_out, norm2_g, ffn_w_gate, ffn_w_up, ffn_conv_w, ffn_w_down, final_g):
    batch, seq, d_model = x.shape
    n_layers = ada_w.shape[0]
    assert seq % SEQ_TILE == 0 and SEQ_TILE % CHUNK == 0 and SEQ_TILE % CONV_ROW_BLOCK == 0
    assert ada_w.shape[2] % MOD_COL_BLOCK == 0

    c_pad = jnp.pad(c, ((0, -batch % SUBLANES), (0, 0)))
    mod = _mod_call(c_pad, ada_w, ada_b)[:, :batch].reshape(n_layers, batch, 1, N_MOD * d_model)

    row = lambda v: v.reshape(1, -1)
    head = jnp.arange(W_GROUP) // HEAD_DIM
    gmat = (head[:, None] == head[None, :]).astype(BF16)
    stacks = (w_in, w_out, ffn_w_gate, ffn_w_up, ffn_w_down)
    weights, w_layer = [w[:1].astype(BF16) for w in stacks], 0
    for l in range(n_layers):
        last = l == n_layers - 1
        sguw = jnp.transpose(sgu_w[l], (1, 0, 2)).reshape(CHUNK, N_HEADS * CHUNK)
        sgub = jnp.repeat(jnp.transpose(sgu_b[l]), HEAD_DIM, axis=1)
        x, weights = _layer_call(
            x, mod[l], w_layer, last, *weights, row(norm1_g[l]),
            conv_a_w[l], row(conv_a_b[l]), row(gn_a_g[l]), row(gn_a_b[l]), gmat,
            _block_diag(pool_w[l]).astype(BF16), row(pool_scale[l]), row(sgu_ln_g[l]),
            row(sgu_ln_b[l]), sguw, sgub, conv_d_w[l], row(norm2_g[l]), ffn_conv_w[l],
            row(final_g), cast_next=() if last else stacks, next_layer=l + 1)
    return x
```

```python
import functools

import jax
import jax.numpy as jnp
from jax import lax
from jax.experimental import pallas as pl
from jax.experimental.pallas import tpu as pltpu

EPS = 1e-6
LOG2_E = 1.4426950408889634
N_MOD = 6
W_GROUP = 256
HEAD_DIM = 64
N_HEADS = W_GROUP // HEAD_DIM
CHUNK = 128
CONV_A_WIDTH = 31
POOL_WINDOWS = (2, 4, 8, 16)
SHORT_CONV_WIDTH = 3
FFN_CONV_WIDTH = 3

SUBLANES = 8
A_HALO = 32
B_HALO = 16
D_HALO = 8
F_HALO = 8
CONV_ROW_BLOCK = 64

SEQ_TILE = 512
MOD_COL_BLOCK = 1536
FFN_COL_BLOCK = 512
N_LAYER_INPUTS = 24
BF16_ROWS = 16
VMEM_LIMIT_BYTES = 56 * 1024 * 1024

F32 = jnp.float32
BF16 = jnp.bfloat16


def _dot(a, b):
    return jnp.dot(a, b, preferred_element_type=F32)


def _sigmoid(v):
    return 1.0 / (1.0 + jnp.exp2(v * (-LOG2_E)))


def _modulated_rms_norm(x, gain, scale, shift):
    ms = jnp.mean(x * x, axis=-1, keepdims=True)
    return (x * lax.rsqrt(ms + EPS)) * (gain * (1.0 + scale)) + shift


def _mod_kernel(*refs, n_cast):
    c_ref, w_ref, b_ref = refs[:3]
    cast_src = refs[3:3 + n_cast]
    o_ref = refs[3 + n_cast]
    cast_dst = refs[4 + n_cast:]
    c = c_ref[...]
    ca = c * _sigmoid(c)
    o_ref[0] = _dot(ca.astype(BF16), w_ref[0].astype(BF16)) + b_ref[0]
    for src_ref, dst_ref in zip(cast_src, cast_dst):
        dst_ref[...] = src_ref[...].astype(BF16)


def _cast_row_block(rows, n_steps):
    block = BF16_ROWS
    while rows % block or rows // block > n_steps:
        block += BF16_ROWS
    return block


def _cast_spec(w, layer, n_steps, step_of):
    rows, cols = w.shape[1:]
    block = _cast_row_block(rows, n_steps)
    return pl.BlockSpec(
        (1, block, cols),
        lambda *idx: (layer, jnp.minimum(step_of(*idx), rows // block - 1), 0))


def _cast_shapes(stacks):
    return [jax.ShapeDtypeStruct((1,) + w.shape[1:], BF16) for w in stacks]


def _mod_call(c_pad, ada_w, ada_b, cast_first):
    n_layers, d_model, n_out = ada_w.shape
    rows = c_pad.shape[0]
    nb = MOD_COL_BLOCK
    col_blocks = n_out // nb
    n_steps = n_layers * col_blocks

    def step_of(l, j):
        return l * col_blocks + j

    outs = pl.pallas_call(
        functools.partial(_mod_kernel, n_cast=len(cast_first)),
        grid=(n_layers, col_blocks),
        in_specs=[
            pl.BlockSpec((rows, d_model), lambda l, j: (0, 0)),
            pl.BlockSpec((1, d_model, nb), lambda l, j: (l, 0, j)),
            pl.BlockSpec((1, 1, nb), lambda l, j: (l, 0, j)),
        ] + [_cast_spec(w, 0, n_steps, step_of) for w in cast_first],
        out_specs=[pl.BlockSpec((1, rows, nb), lambda l, j: (l, 0, j))]
        + [_cast_spec(w, 0, n_steps, step_of) for w in cast_first],
        out_shape=[jax.ShapeDtypeStruct((n_layers, rows, n_out), F32)] + _cast_shapes(cast_first),
        compiler_params=pltpu.CompilerParams(
            dimension_semantics=("arbitrary", "arbitrary"),
            vmem_limit_bytes=VMEM_LIMIT_BYTES),
        name="mod",
    )(c_pad, ada_w, ada_b.reshape(n_layers, 1, n_out), *cast_first)
    return outs[0], outs[1:]


def _causal_conv31(glu, abuf, caw_ref, cab_ref, ts):
    abuf[pl.ds(A_HALO, ts), :] = glu
    rb = CONV_ROW_BLOCK
    first_off = A_HALO - (CONV_A_WIDTH - 1)
    blocks = []
    for r0 in range(0, ts, rb):
        acc = None
        for b in range(SUBLANES):
            taps = [(o - first_off, o - b) for o in range(first_off, A_HALO + 1)
                    if o % SUBLANES == b]
            span = max(off for _, off in taps)
            win = abuf[pl.ds(r0 + b, rb + span), :]
            part = None
            for k, off in taps:
                term = caw_ref[k:k + 1, :] * win[off:off + rb]
                part = term if part is None else part + term
            acc = part if acc is None else acc + part
        blocks.append(acc)
    conv = jnp.concatenate(blocks, axis=0) + cab_ref[...]
    abuf[pl.ds(0, A_HALO), :] = abuf[pl.ds(ts, A_HALO), :]
    return conv


def _group_norm_silu(conv, gng_ref, gnb_ref, gmat):
    inv = 1.0 / HEAD_DIM
    conv_hi = conv.astype(BF16)
    conv_lo = (conv - conv_hi.astype(F32)).astype(BF16)
    mu = (_dot(conv_hi, gmat) + _dot(conv_lo, gmat)) * inv
    d = conv - mu
    var = _dot((d * d).astype(BF16), gmat) * inv
    y = d * lax.rsqrt(var + EPS) * gng_ref[...] + gnb_ref[...]
    return y * _sigmoid(y)


def _pool_mixer(pb, bbuf, poolw_ref, pools_ref, ts, is_first):
    bbuf[pl.ds(B_HALO, ts), :] = pb
    half = W_GROUP // 2
    ext = bbuf[pl.ds(0, ts + B_HALO), :]
    s2 = ext + pltpu.roll(ext, 1, 0)
    s4 = s2 + pltpu.roll(s2, 2, 0)
    s4_hi = s4[:, half:W_GROUP]
    s8 = s4_hi + pltpu.roll(s4_hi, 4, 0)
    s16 = s8[B_HALO:] + s8[B_HALO - 8:B_HALO - 8 + ts]
    bbuf[pl.ds(0, B_HALO), :] = bbuf[pl.ds(ts, B_HALO), :]

    low = lax.broadcasted_iota(jnp.int32, (1, half), 1) < (half // 2)
    sum_a = jnp.where(low, s2[B_HALO:, 0:half], s4[B_HALO:, 0:half])
    sum_b = jnp.where(low, s8[B_HALO:], s16)
    sums = jnp.concatenate([sum_a, sum_b], axis=-1)

    lane_full = lax.broadcasted_iota(jnp.int32, (1, W_GROUP), 1)
    win = jnp.where(lane_full < 64, 2.0,
                    jnp.where(lane_full < 128, 4.0, jnp.where(lane_full < 192, 8.0, 16.0)))
    pos = lax.broadcasted_iota(jnp.int32, (B_HALO, W_GROUP), 0).astype(F32) + 1.0
    count = jnp.where(is_first, jnp.minimum(pos, win), win)
    mean = jnp.concatenate([sums[0:B_HALO] / count, sums[B_HALO:] * (1.0 / win)], axis=0)
    y = mean - pb
    return _dot(y.astype(BF16), poolw_ref[...]) * pools_ref[...]


def _spatial_gating(u, v, lng_ref, lnb_ref, sguw_ref, sgub_ref, ts):
    mu = jnp.mean(v, axis=-1, keepdims=True)
    d = v - mu
    var = jnp.mean(d * d, axis=-1, keepdims=True)
    vn = (d * lax.rsqrt(var + EPS) * lng_ref[...] + lnb_ref[...]).astype(BF16)

    row = lax.broadcasted_iota(jnp.int32, (CHUNK, N_HEADS * CHUNK), 0)
    col = lax.broadcasted_iota(jnp.int32, (CHUNK, N_HEADS * CHUNK), 1)
    wcat = jnp.where((col & (CHUNK - 1)) <= row, sguw_ref[...], 0.0).astype(BF16)

    head_of_lane = lax.broadcasted_iota(jnp.int32, (CHUNK, W_GROUP), 1) // HEAD_DIM
    zero = jnp.zeros((CHUNK, W_GROUP), BF16)
    outs = []
    for c0 in range(0, ts, CHUNK):
        vc = vn[c0:c0 + CHUNK]
        vstack = jnp.concatenate(
            [jnp.where(head_of_lane == h, vc, zero) for h in range(N_HEADS)], axis=0)
        outs.append(_dot(wcat, vstack) + sgub_ref[...])
    return u * jnp.concatenate(outs, axis=0)


def _short_conv(bg, cgh, dbuf, cdw_ref, ts):
    dbuf[pl.ds(D_HALO, ts), :] = cgh
    conv = cdw_ref[SHORT_CONV_WIDTH - 1:SHORT_CONV_WIDTH, :] * cgh
    for k in range(SHORT_CONV_WIDTH - 1):
        back = SHORT_CONV_WIDTH - 1 - k
        conv = conv + cdw_ref[k:k + 1, :] * dbuf[pl.ds(D_HALO - back, ts), :]
    dbuf[pl.ds(0, D_HALO), :] = dbuf[pl.ds(ts, D_HALO), :]
    return bg * conv


def _ffn_gated(h, wg_ref, wu_ref, fcw_ref, gcarry, mbuf, ts, col_blocks):
    for c0, c1 in col_blocks:
        cols = slice(c0, c1)
        g = _dot(h, wg_ref[0, :, cols])
        u = _dot(h, wu_ref[0, :, cols])
        ext = jnp.concatenate([gcarry[:, cols], g], axis=0)
        conv = fcw_ref[FFN_CONV_WIDTH - 1:FFN_CONV_WIDTH, cols] * g
        for k in range(FFN_CONV_WIDTH - 1):
            back = FFN_CONV_WIDTH - 1 - k
            conv = conv + fcw_ref[k:k + 1, cols] * pltpu.roll(ext, back, 0)[F_HALO:]
        gcarry[:, cols] = g[ts - F_HALO:]
        mbuf[:, cols] = (conv * _sigmoid(conv) * u).astype(BF16)


def _layer_kernel(*refs, ts, d_model, d_ff, tiles_per_seq, final_norm, n_cast):
    (x_ref, modc_ref, modp_ref, n1g_ref, w_in_ref, caw_ref, cab_ref, gng_ref, gnb_ref, gmat_ref,
     poolw_ref, pools_ref, lng_ref, lnb_ref, sguw_ref, sgub_ref, cdw_ref, w_out_ref, n2g_ref,
     wg_ref, wu_ref, fcw_ref, wd_ref, fing_ref) = refs[:N_LAYER_INPUTS]
    cast_src = refs[N_LAYER_INPUTS:N_LAYER_INPUTS + n_cast]
    o_ref = refs[N_LAYER_INPUTS + n_cast]
    cast_dst = refs[N_LAYER_INPUTS + n_cast + 1:N_LAYER_INPUTS + 2 * n_cast + 1]
    abuf, bbuf, dbuf, ybuf, hbuf, xmid, gcarry, mbuf = refs[N_LAYER_INPUTS + 2 * n_cast + 1:]

    for src_ref, dst_ref in zip(cast_src, cast_dst):
        dst_ref[...] = src_ref[...].astype(BF16)

    step = pl.program_id(0)
    mix_first = lax.rem(step, tiles_per_seq) == 0
    ffn_first = (step == 0) | (lax.rem(step - 1, tiles_per_seq) == 0)

    @pl.when(step == 0)
    def _():
        hbuf[...] = jnp.zeros(hbuf.shape, BF16)
        xmid[...] = jnp.zeros(xmid.shape, F32)

    @pl.when(mix_first)
    def _():
        abuf[pl.ds(0, A_HALO), :] = jnp.zeros((A_HALO, W_GROUP), F32)
        bbuf[pl.ds(0, B_HALO), :] = jnp.zeros((B_HALO, W_GROUP), F32)
        dbuf[pl.ds(0, D_HALO), :] = jnp.zeros((D_HALO, W_GROUP), F32)

    @pl.when(ffn_first)
    def _():
        gcarry[...] = jnp.zeros(gcarry.shape, F32)

    modc = modc_ref[0]
    modp = modp_ref[0]

    def part(mod, i):
        return mod[:, i * d_model:(i + 1) * d_model]

    x_res = xmid[...]
    col_blocks = [(c0, min(c0 + FFN_COL_BLOCK, d_ff)) for c0 in range(0, d_ff, FFN_COL_BLOCK)]
    _ffn_gated(hbuf[...], wg_ref, wu_ref, fcw_ref, gcarry, mbuf, ts, col_blocks)

    x = x_ref[0]
    h1 = _modulated_rms_norm(x, n1g_ref[...], part(modc, 1), part(modc, 0)).astype(BF16)

    def proj(first, n):
        return _dot(h1, w_in_ref[0, :, first * W_GROUP:(first + n) * W_GROUP])

    def col(p, i):
        return p[:, i * W_GROUP:(i + 1) * W_GROUP]

    pa = proj(0, 2)
    conv = _causal_conv31(col(pa, 0) * _sigmoid(col(pa, 1)), abuf, caw_ref, cab_ref, ts)
    pb = proj(2, 1)
    pc = proj(3, 2)
    pd = proj(5, 3)

    f = _dot(mbuf[...], wd_ref[0])

    yb = _pool_mixer(pb, bbuf, poolw_ref, pools_ref, ts, mix_first)
    ybuf[:, 0:W_GROUP] = yb.astype(BF16)
    yc = _spatial_gating(col(pc, 0), col(pc, 1), lng_ref, lnb_ref, sguw_ref, sgub_ref, ts)
    ybuf[:, W_GROUP:2 * W_GROUP] = yc.astype(BF16)
    yd = _short_conv(col(pd, 0), col(pd, 1) * col(pd, 2), dbuf, cdw_ref, ts)
    ybuf[:, 2 * W_GROUP:3 * W_GROUP] = yd.astype(BF16)
    y = _dot(ybuf[...], w_out_ref[0, W_GROUP:4 * W_GROUP, :])
    ya = _group_norm_silu(conv, gng_ref, gnb_ref, gmat_ref[...])
    y = y + _dot(ya.astype(BF16), w_out_ref[0, 0:W_GROUP, :])
    x_mid = x + part(modc, 2) * y
    xmid[...] = x_mid
    hbuf[...] = _modulated_rms_norm(x_mid, n2g_ref[...], part(modc, 4), part(modc, 3)).astype(BF16)

    out = x_res + part(modp, 5) * f
    if final_norm:
        ms = jnp.mean(out * out, axis=-1, keepdims=True)
        out = out * lax.rsqrt(ms + EPS) * fing_ref[...]
    o_ref[0] = out


def _whole(shape):
    zeros = (0,) * len(shape)
    return pl.BlockSpec(shape, lambda *_: zeros, pipeline_mode=pl.Buffered(1))


def _layer(stacked, layer):
    index = (layer,) + (0,) * (stacked.ndim - 1)
    return pl.BlockSpec((1,) + stacked.shape[1:], lambda *_: index, pipeline_mode=pl.Buffered(1))


def _layer_call(x, mod_l, layer, final_norm, w_in, w_out, wg, wu, wd, n1g, caw, cab, gng, gnb, gmat,
                poolw, pools, lng, lnb, sguw, sgub, cdw, n2g, fcw, fing, cast_next, next_layer):
    batch, seq, d_model = x.shape
    d_ff = wg.shape[2]
    ts = SEQ_TILE
    tiles_per_seq = seq // ts
    n_tiles = batch * tiles_per_seq

    def cur(i):
        return jnp.minimum(i, n_tiles - 1)

    def prev(i):
        return jnp.maximum(i - 1, 0)

    def x_spec(tile):
        return pl.BlockSpec((1, ts, d_model),
                            lambda i: (tile(i) // tiles_per_seq, tile(i) % tiles_per_seq, 0))

    def mod_spec(tile):
        return pl.BlockSpec((1, 1, N_MOD * d_model), lambda i: (tile(i) // tiles_per_seq, 0, 0))

    def cast_spec(w, which):
        return _cast_spec(w, which, n_tiles, lambda i: i)

    mix_small = (caw, cab, gng, gnb, gmat, poolw, pools, lng, lnb, sguw, sgub, cdw)
    outs = pl.pallas_call(
        functools.partial(_layer_kernel, ts=ts, d_model=d_model, d_ff=d_ff,
                          tiles_per_seq=tiles_per_seq, final_norm=final_norm,
                          n_cast=len(cast_next)),
        grid=(n_tiles + 1,),
        in_specs=[x_spec(cur), mod_spec(cur), mod_spec(prev), _whole(n1g.shape), _layer(w_in, layer)]
        + [_whole(p.shape) for p in mix_small]
        + [_layer(w_out, layer), _whole(n2g.shape), _layer(wg, layer), _layer(wu, layer),
           _whole(fcw.shape), _layer(wd, layer), _whole(fing.shape)]
        + [cast_spec(w, next_layer) for w in cast_next],
        out_specs=[x_spec(prev)] + [cast_spec(w, 0) for w in cast_next],
        out_shape=[jax.ShapeDtypeStruct(x.shape, x.dtype)]
        + _cast_shapes(cast_next),
        scratch_shapes=[
            pltpu.VMEM((ts + A_HALO, W_GROUP), F32),
            pltpu.VMEM((ts + B_HALO, W_GROUP), F32),
            pltpu.VMEM((ts + D_HALO, W_GROUP), F32),
            pltpu.VMEM((ts, 3 * W_GROUP), BF16),
            pltpu.VMEM((ts, d_model), BF16),
            pltpu.VMEM((ts, d_model), F32),
            pltpu.VMEM((F_HALO, d_ff), F32),
            pltpu.VMEM((ts, d_ff), BF16),
        ],
        compiler_params=pltpu.CompilerParams(
            dimension_semantics=("arbitrary",),
            vmem_limit_bytes=VMEM_LIMIT_BYTES),
        name="layer",
    )(x, mod_l, mod_l, n1g, w_in, *mix_small, w_out, n2g, wg, wu, fcw, wd, fing, *cast_next)
    return outs[0], outs[1:]


def _block_diag(blocks):
    g, n, _ = blocks.shape
    eye = jnp.eye(g, dtype=blocks.dtype)
    return jnp.einsum("gij,gh->gihj", blocks, eye).reshape(g * n, g * n)


def kernel(x, c, norm1_g, ada_w, ada_b, w_in, conv_a_w, conv_a_b, gn_a_g, gn_a_b, pool_w, pool_scale, sgu_ln_g, sgu_ln_b, sgu_w, sgu_b, conv_d_w, w_out, norm2_g, ffn_w_gate, ffn_w_up, ffn_conv_w, ffn_w_down, final_g):
    batch, seq, d_model = x.shape
    n_layers = ada_w.shape[0]
    assert seq % SEQ_TILE == 0 and SEQ_TILE % CHUNK == 0 and SEQ_TILE % CONV_ROW_BLOCK == 0
    assert ada_w.shape[2] % MOD_COL_BLOCK == 0

    c_pad = jnp.pad(c, ((0, -batch % SUBLANES), (0, 0)))
    stacks = (w_in, w_out, ffn_w_gate, ffn_w_up, ffn_w_down)
    mod, weights = _mod_call(c_pad, ada_w, ada_b, stacks)
    mod = mod[:, :batch].reshape(n_layers, batch, 1, N_MOD * d_model)

    row = lambda v: v.reshape(1, -1)
    head = jnp.arange(W_GROUP) // HEAD_DIM
    gmat = (head[:, None] == head[None, :]).astype(BF16)
    for l in range(n_layers):
        last = l == n_layers - 1
        sguw = jnp.transpose(sgu_w[l], (1, 0, 2)).reshape(CHUNK, N_HEADS * CHUNK)
        sgub = jnp.repeat(jnp.transpose(sgu_b[l]), HEAD_DIM, axis=1)
        x, weights = _layer_call(
            x, mod[l], 0, last, *weights, row(norm1_g[l]),
            conv_a_w[l], row(conv_a_b[l]), row(gn_a_g[l]), row(gn_a_b[l]), gmat,
            _block_diag(pool_w[l]).astype(BF16), row(pool_scale[l]), row(sgu_ln_g[l]),
            row(sgu_ln_b[l]), sguw, sgub, conv_d_w[l], row(norm2_g[l]), ffn_conv_w[l],
            row(final_g), cast_next=() if last else stacks, next_layer=l + 1)
    return x
```

```python
import functools

import jax
import jax.numpy as jnp
from jax import lax
from jax.experimental import pallas as pl
from jax.experimental.pallas import tpu as pltpu

EPS = 1e-6
LOG2_E = 1.4426950408889634
N_MOD = 6
W_GROUP = 256
HEAD_DIM = 64
N_HEADS = W_GROUP // HEAD_DIM
CHUNK = 128
CONV_A_WIDTH = 31
POOL_WINDOWS = (2, 4, 8, 16)
SHORT_CONV_WIDTH = 3
FFN_CONV_WIDTH = 3

SUBLANES = 8
A_HALO = 32
B_HALO = 16
D_HALO = 8
F_HALO = 8
CONV_ROW_BLOCK = 64

SEQ_TILE = 512
MOD_COL_BLOCK = 1536
FFN_COL_BLOCK = 512
N_LAYER_INPUTS = 18
N_GROUP_VECTORS = 6
BF16_ROWS = 16
VMEM_LIMIT_BYTES = 56 * 1024 * 1024

F32 = jnp.float32
BF16 = jnp.bfloat16


def _dot(a, b):
    return jnp.dot(a, b, preferred_element_type=F32)


def _sigmoid(v):
    return 1.0 / (1.0 + jnp.exp2(v * (-LOG2_E)))


def _modulated_rms_norm(x, gain, scale, shift):
    ms = jnp.mean(x * x, axis=-1, keepdims=True)
    return (x * lax.rsqrt(ms + EPS)) * (gain * (1.0 + scale)) + shift


def _mod_kernel(*refs, n_cast):
    c_ref, w_ref, b_ref = refs[:3]
    cast_src = refs[3:3 + n_cast]
    o_ref = refs[3 + n_cast]
    cast_dst = refs[4 + n_cast:]
    c = c_ref[...]
    ca = c * _sigmoid(c)
    o_ref[0] = _dot(ca.astype(BF16), w_ref[0].astype(BF16)) + b_ref[0]
    for src_ref, dst_ref in zip(cast_src, cast_dst):
        dst_ref[...] = src_ref[...].astype(BF16)


def _cast_row_block(rows, n_steps):
    block = BF16_ROWS
    while rows % block or rows // block > n_steps:
        block += BF16_ROWS
    return block


def _cast_spec(w, layer, n_steps, step_of):
    rows, cols = w.shape[1:]
    block = _cast_row_block(rows, n_steps)
    return pl.BlockSpec(
        (1, block, cols),
        lambda *idx: (layer, jnp.minimum(step_of(*idx), rows // block - 1), 0))


def _cast_shapes(stacks):
    return [jax.ShapeDtypeStruct((1,) + w.shape[1:], BF16) for w in stacks]


def _mod_call(c_pad, ada_w, ada_b, cast_first):
    n_layers, d_model, n_out = ada_w.shape
    rows = c_pad.shape[0]
    nb = MOD_COL_BLOCK
    col_blocks = n_out // nb
    n_steps = n_layers * col_blocks

    def step_of(l, j):
        return l * col_blocks + j

    outs = pl.pallas_call(
        functools.partial(_mod_kernel, n_cast=len(cast_first)),
        grid=(n_layers, col_blocks),
        in_specs=[
            pl.BlockSpec((rows, d_model), lambda l, j: (0, 0)),
            pl.BlockSpec((1, d_model, nb), lambda l, j: (l, 0, j)),
            pl.BlockSpec((1, 1, nb), lambda l, j: (l, 0, j)),
        ] + [_cast_spec(w, 0, n_steps, step_of) for w in cast_first],
        out_specs=[pl.BlockSpec((1, rows, nb), lambda l, j: (l, 0, j))]
        + [_cast_spec(w, 0, n_steps, step_of) for w in cast_first],
        out_shape=[jax.ShapeDtypeStruct((n_layers, rows, n_out), F32)] + _cast_shapes(cast_first),
        compiler_params=pltpu.CompilerParams(
            dimension_semantics=("arbitrary", "arbitrary"),
            vmem_limit_bytes=VMEM_LIMIT_BYTES),
        name="mod",
    )(c_pad, ada_w, ada_b.reshape(n_layers, 1, n_out), *cast_first)
    return outs[0], outs[1:]


def _causal_conv31(glu, abuf, caw_ref, cab_ref, ts):
    abuf[pl.ds(A_HALO, ts), :] = glu
    rb = CONV_ROW_BLOCK
    first_off = A_HALO - (CONV_A_WIDTH - 1)
    blocks = []
    for r0 in range(0, ts, rb):
        acc = None
        for b in range(SUBLANES):
            taps = [(o - first_off, o - b) for o in range(first_off, A_HALO + 1)
                    if o % SUBLANES == b]
            span = max(off for _, off in taps)
            win = abuf[pl.ds(r0 + b, rb + span), :]
            part = None
            for k, off in taps:
                term = caw_ref[k:k + 1, :] * win[off:off + rb]
                part = term if part is None else part + term
            acc = part if acc is None else acc + part
        blocks.append(acc)
    conv = jnp.concatenate(blocks, axis=0) + cab_ref[...]
    abuf[pl.ds(0, A_HALO), :] = abuf[pl.ds(ts, A_HALO), :]
    return conv


def _group_norm_silu(conv, gng_ref, gnb_ref, gmat):
    inv = 1.0 / HEAD_DIM
    conv_hi = conv.astype(BF16)
    conv_lo = (conv - conv_hi.astype(F32)).astype(BF16)
    mu = (_dot(conv_hi, gmat) + _dot(conv_lo, gmat)) * inv
    d = conv - mu
    var = _dot((d * d).astype(BF16), gmat) * inv
    y = d * lax.rsqrt(var + EPS) * gng_ref[...] + gnb_ref[...]
    return y * _sigmoid(y)


def _pool_mixer(pb, bbuf, poolw_ref, pools_ref, ts, is_first):
    bbuf[pl.ds(B_HALO, ts), :] = pb
    half = W_GROUP // 2
    ext = bbuf[pl.ds(0, ts + B_HALO), :]
    s2 = ext + pltpu.roll(ext, 1, 0)
    s4 = s2 + pltpu.roll(s2, 2, 0)
    s4_hi = s4[:, half:W_GROUP]
    s8 = s4_hi + pltpu.roll(s4_hi, 4, 0)
    s16 = s8[B_HALO:] + s8[B_HALO - 8:B_HALO - 8 + ts]
    bbuf[pl.ds(0, B_HALO), :] = bbuf[pl.ds(ts, B_HALO), :]

    low = lax.broadcasted_iota(jnp.int32, (1, half), 1) < (half // 2)
    sum_a = jnp.where(low, s2[B_HALO:, 0:half], s4[B_HALO:, 0:half])
    sum_b = jnp.where(low, s8[B_HALO:], s16)
    sums = jnp.concatenate([sum_a, sum_b], axis=-1)

    lane_full = lax.broadcasted_iota(jnp.int32, (1, W_GROUP), 1)
    win = jnp.where(lane_full < 64, 2.0,
                    jnp.where(lane_full < 128, 4.0, jnp.where(lane_full < 192, 8.0, 16.0)))
    pos = lax.broadcasted_iota(jnp.int32, (B_HALO, W_GROUP), 0).astype(F32) + 1.0
    count = jnp.where(is_first, jnp.minimum(pos, win), win)
    mean = jnp.concatenate([sums[0:B_HALO] / count, sums[B_HALO:] * (1.0 / win)], axis=0)
    y = mean - pb
    return _dot(y.astype(BF16), poolw_ref[...]) * pools_ref[...]


def _spatial_gating(u, v, lng_ref, lnb_ref, sguw_ref, sgub_ref, ts):
    mu = jnp.mean(v, axis=-1, keepdims=True)
    d = v - mu
    var = jnp.mean(d * d, axis=-1, keepdims=True)
    vn = (d * lax.rsqrt(var + EPS) * lng_ref[...] + lnb_ref[...]).astype(BF16)

    row = lax.broadcasted_iota(jnp.int32, (CHUNK, N_HEADS * CHUNK), 0)
    col = lax.broadcasted_iota(jnp.int32, (CHUNK, N_HEADS * CHUNK), 1)
    wcat = jnp.where((col & (CHUNK - 1)) <= row, sguw_ref[...], 0.0).astype(BF16)

    head_of_lane = lax.broadcasted_iota(jnp.int32, (CHUNK, W_GROUP), 1) // HEAD_DIM
    zero = jnp.zeros((CHUNK, W_GROUP), BF16)
    outs = []
    for c0 in range(0, ts, CHUNK):
        vc = vn[c0:c0 + CHUNK]
        vstack = jnp.concatenate(
            [jnp.where(head_of_lane == h, vc, zero) for h in range(N_HEADS)], axis=0)
        outs.append(_dot(wcat, vstack) + sgub_ref[...])
    return u * jnp.concatenate(outs, axis=0)


def _short_conv(bg, cgh, dbuf, cdw_ref, ts):
    dbuf[pl.ds(D_HALO, ts), :] = cgh
    conv = cdw_ref[SHORT_CONV_WIDTH - 1:SHORT_CONV_WIDTH, :] * cgh
    for k in range(SHORT_CONV_WIDTH - 1):
        back = SHORT_CONV_WIDTH - 1 - k
        conv = conv + cdw_ref[k:k + 1, :] * dbuf[pl.ds(D_HALO - back, ts), :]
    dbuf[pl.ds(0, D_HALO), :] = dbuf[pl.ds(ts, D_HALO), :]
    return bg * conv


def _ffn_gated(h, wg_ref, wu_ref, fcw_ref, gcarry, mbuf, ts, col_blocks):
    for c0, c1 in col_blocks:
        cols = slice(c0, c1)
        g = _dot(h, wg_ref[0, :, cols])
        u = _dot(h, wu_ref[0, :, cols])
        ext = jnp.concatenate([gcarry[:, cols], g], axis=0)
        conv = fcw_ref[FFN_CONV_WIDTH - 1:FFN_CONV_WIDTH, cols] * g
        for k in range(FFN_CONV_WIDTH - 1):
            back = FFN_CONV_WIDTH - 1 - k
            conv = conv + fcw_ref[k:k + 1, cols] * pltpu.roll(ext, back, 0)[F_HALO:]
        gcarry[:, cols] = g[ts - F_HALO:]
        mbuf[:, cols] = (conv * _sigmoid(conv) * u).astype(BF16)


def _layer_kernel(*refs, ts, d_model, d_ff, tiles_per_seq, final_norm, n_cast):
    (x_ref, modc_ref, modp_ref, norms_ref, vec_ref, caw_ref, cdw_ref, fcw_ref, gmat_ref, poolw_ref,
     sguw_ref, sgub_ref, fing_ref, w_in_ref, w_out_ref, wg_ref, wu_ref, wd_ref) = refs[:N_LAYER_INPUTS]
    caw_ref, cdw_ref, fcw_ref = caw_ref.at[0], cdw_ref.at[0], fcw_ref.at[0]
    poolw_ref, sguw_ref, sgub_ref = poolw_ref.at[0], sguw_ref.at[0], sgub_ref.at[0]
    n1g_ref, n2g_ref = (norms_ref.at[0, pl.ds(k, 1)] for k in range(2))
    cab_ref, gng_ref, gnb_ref, pools_ref, lng_ref, lnb_ref = (
        vec_ref.at[0, pl.ds(k, 1)] for k in range(N_GROUP_VECTORS))
    cast_src = refs[N_LAYER_INPUTS:N_LAYER_INPUTS + n_cast]
    o_ref = refs[N_LAYER_INPUTS + n_cast]
    cast_dst = refs[N_LAYER_INPUTS + n_cast + 1:N_LAYER_INPUTS + 2 * n_cast + 1]
    abuf, bbuf, dbuf, ybuf, hbuf, xmid, gcarry, mbuf = refs[N_LAYER_INPUTS + 2 * n_cast + 1:]

    for src_ref, dst_ref in zip(cast_src, cast_dst):
        dst_ref[...] = src_ref[...].astype(BF16)

    step = pl.program_id(0)
    mix_first = lax.rem(step, tiles_per_seq) == 0
    ffn_first = (step == 0) | (lax.rem(step - 1, tiles_per_seq) == 0)

    @pl.when(step == 0)
    def _():
        hbuf[...] = jnp.zeros(hbuf.shape, BF16)
        xmid[...] = jnp.zeros(xmid.shape, F32)

    @pl.when(mix_first)
    def _():
        abuf[pl.ds(0, A_HALO), :] = jnp.zeros((A_HALO, W_GROUP), F32)
        bbuf[pl.ds(0, B_HALO), :] = jnp.zeros((B_HALO, W_GROUP), F32)
        dbuf[pl.ds(0, D_HALO), :] = jnp.zeros((D_HALO, W_GROUP), F32)

    @pl.when(ffn_first)
    def _():
        gcarry[...] = jnp.zeros(gcarry.shape, F32)

    modc = modc_ref[0]
    modp = modp_ref[0]

    def part(mod, i):
        return mod[:, i * d_model:(i + 1) * d_model]

    x_res = xmid[...]
    col_blocks = [(c0, min(c0 + FFN_COL_BLOCK, d_ff)) for c0 in range(0, d_ff, FFN_COL_BLOCK)]
    _ffn_gated(hbuf[...], wg_ref, wu_ref, fcw_ref, gcarry, mbuf, ts, col_blocks)

    x = x_ref[0]
    h1 = _modulated_rms_norm(x, n1g_ref[...], part(modc, 1), part(modc, 0)).astype(BF16)

    def proj(first, n):
        return _dot(h1, w_in_ref[0, :, first * W_GROUP:(first + n) * W_GROUP])

    def col(p, i):
        return p[:, i * W_GROUP:(i + 1) * W_GROUP]

    pa = proj(0, 2)
    conv = _causal_conv31(col(pa, 0) * _sigmoid(col(pa, 1)), abuf, caw_ref, cab_ref, ts)
    pb = proj(2, 1)
    pc = proj(3, 2)
    pd = proj(5, 3)

    f = _dot(mbuf[...], wd_ref[0])

    yb = _pool_mixer(pb, bbuf, poolw_ref, pools_ref, ts, mix_first)
    ybuf[:, 0:W_GROUP] = yb.astype(BF16)
    yc = _spatial_gating(col(pc, 0), col(pc, 1), lng_ref, lnb_ref, sguw_ref, sgub_ref, ts)
    ybuf[:, W_GROUP:2 * W_GROUP] = yc.astype(BF16)
    yd = _short_conv(col(pd, 0), col(pd, 1) * col(pd, 2), dbuf, cdw_ref, ts)
    ybuf[:, 2 * W_GROUP:3 * W_GROUP] = yd.astype(BF16)
    y = _dot(ybuf[...], w_out_ref[0, W_GROUP:4 * W_GROUP, :])
    ya = _group_norm_silu(conv, gng_ref, gnb_ref, gmat_ref[...])
    y = y + _dot(ya.astype(BF16), w_out_ref[0, 0:W_GROUP, :])
    x_mid = x + part(modc, 2) * y
    xmid[...] = x_mid
    hbuf[...] = _modulated_rms_norm(x_mid, n2g_ref[...], part(modc, 4), part(modc, 3)).astype(BF16)

    out = x_res + part(modp, 5) * f
    if final_norm:
        ms = jnp.mean(out * out, axis=-1, keepdims=True)
        out = out * lax.rsqrt(ms + EPS) * fing_ref[...]
    o_ref[0] = out


def _whole(shape):
    zeros = (0,) * len(shape)
    return pl.BlockSpec(shape, lambda *_: zeros, pipeline_mode=pl.Buffered(1))


def _layer(stacked, layer):
    index = (layer,) + (0,) * (stacked.ndim - 1)
    return pl.BlockSpec((1,) + stacked.shape[1:], lambda *_: index, pipeline_mode=pl.Buffered(1))


def _layer_call(x, mod_l, layer, final_norm, weights, norms, vecs, caw, cdw, fcw, gmat, poolw, sguw,
                sgub, fing, cast_next):
    batch, seq, d_model = x.shape
    w_in, w_out, wg, wu, wd = weights
    d_ff = wg.shape[2]
    ts = SEQ_TILE
    tiles_per_seq = seq // ts
    n_tiles = batch * tiles_per_seq

    def cur(i):
        return jnp.minimum(i, n_tiles - 1)

    def prev(i):
        return jnp.maximum(i - 1, 0)

    def x_spec(tile):
        return pl.BlockSpec((1, ts, d_model),
                            lambda i: (tile(i) // tiles_per_seq, tile(i) % tiles_per_seq, 0))

    def mod_spec(tile):
        return pl.BlockSpec((1, 1, N_MOD * d_model), lambda i: (tile(i) // tiles_per_seq, 0, 0))

    def cast_spec(w, which):
        return _cast_spec(w, which, n_tiles, lambda i: i)

    per_layer = (norms, vecs, caw, cdw, fcw)
    outs = pl.pallas_call(
        functools.partial(_layer_kernel, ts=ts, d_model=d_model, d_ff=d_ff,
                          tiles_per_seq=tiles_per_seq, final_norm=final_norm,
                          n_cast=len(cast_next)),
        grid=(n_tiles + 1,),
        in_specs=[x_spec(cur), mod_spec(cur), mod_spec(prev)]
        + [_layer(p, layer) for p in per_layer] + [_whole(gmat.shape)]
        + [_layer(p, layer) for p in (poolw, sguw, sgub)] + [_whole(fing.shape)]
        + [_layer(w, 0) for w in weights]
        + [cast_spec(w, layer + 1) for w in cast_next],
        out_specs=[x_spec(prev)] + [cast_spec(w, 0) for w in cast_next],
        out_shape=[jax.ShapeDtypeStruct(x.shape, x.dtype)] + _cast_shapes(cast_next),
        scratch_shapes=[
            pltpu.VMEM((ts + A_HALO, W_GROUP), F32),
            pltpu.VMEM((ts + B_HALO, W_GROUP), F32),
            pltpu.VMEM((ts + D_HALO, W_GROUP), F32),
            pltpu.VMEM((ts, 3 * W_GROUP), BF16),
            pltpu.VMEM((ts, d_model), BF16),
            pltpu.VMEM((ts, d_model), F32),
            pltpu.VMEM((F_HALO, d_ff), F32),
            pltpu.VMEM((ts, d_ff), BF16),
        ],
        compiler_params=pltpu.CompilerParams(
            dimension_semantics=("arbitrary",),
            vmem_limit_bytes=VMEM_LIMIT_BYTES),
        name="layer",
    )(x, mod_l, mod_l, *per_layer, gmat, poolw, sguw, sgub, fing, *weights, *cast_next)
    return outs[0], outs[1:]


def _block_diag(blocks):
    layers, g, n, _ = blocks.shape
    eye = jnp.eye(g, dtype=blocks.dtype)
    return jnp.einsum("lgij,gh->lgihj", blocks, eye).reshape(layers, g * n, g * n)


def kernel(x, c, norm1_g, ada_w, ada_b, w_in, conv_a_w, conv_a_b, gn_a_g, gn_a_b, pool_w, pool_scale, sgu_ln_g, sgu_ln_b, sgu_w, sgu_b, conv_d_w, w_out, norm2_g, ffn_w_gate, ffn_w_up, ffn_conv_w, ffn_w_down, final_g):
    batch, seq, d_model = x.shape
    n_layers = ada_w.shape[0]
    assert seq % SEQ_TILE == 0 and SEQ_TILE % CHUNK == 0 and SEQ_TILE % CONV_ROW_BLOCK == 0
    assert ada_w.shape[2] % MOD_COL_BLOCK == 0

    stacks = (w_in, w_out, ffn_w_gate, ffn_w_up, ffn_w_down)
    c_pad = jnp.pad(c, ((0, -batch % SUBLANES), (0, 0)))
    mod, weights = _mod_call(c_pad, ada_w, ada_b, stacks)
    mod = mod[:, :batch].reshape(n_layers, batch, 1, N_MOD * d_model)

    norms = jnp.stack([norm1_g, norm2_g], axis=1)
    vecs = jnp.stack([conv_a_b, gn_a_g, gn_a_b, pool_scale, sgu_ln_g, sgu_ln_b], axis=1)
    head = jnp.arange(W_GROUP) // HEAD_DIM
    gmat = (head[:, None] == head[None, :]).astype(BF16)
    poolw = _block_diag(pool_w).astype(BF16)
    sguw = jnp.transpose(sgu_w, (0, 2, 1, 3)).reshape(n_layers, CHUNK, N_HEADS * CHUNK)
    sgub = jnp.repeat(jnp.transpose(sgu_b, (0, 2, 1)), HEAD_DIM, axis=2)
    fing = final_g.reshape(1, -1)
    for l in range(n_layers):
        last = l == n_layers - 1
        x, weights = _layer_call(
            x, mod[l], l, last, weights, norms, vecs, conv_a_w, conv_d_w, ffn_conv_w, gmat, poolw,
            sguw, sgub, fing, cast_next=() if last else stacks)
    return x
```

```python
import functools

import jax
import jax.numpy as jnp
from jax import lax
from jax.experimental import pallas as pl
from jax.experimental.pallas import tpu as pltpu

EPS = 1e-6
LOG2_E = 1.4426950408889634
N_MOD = 6
W_GROUP = 256
HEAD_DIM = 64
N_HEADS = W_GROUP // HEAD_DIM
CHUNK = 128
CONV_A_WIDTH = 31
POOL_WINDOWS = (2, 4, 8, 16)
SHORT_CONV_WIDTH = 3
FFN_CONV_WIDTH = 3

SUBLANES = 8
A_HALO = 32
B_HALO = 16
D_HALO = 8
F_HALO = 8
CONV_ROW_BLOCK = 64

SEQ_TILE = 512
MOD_COL_BLOCK = 1536
FFN_COL_BLOCK = 512
N_LAYER_INPUTS = 24
N_LAYER_VECTORS = 8
BF16_ROWS = 16
VMEM_LIMIT_BYTES = 56 * 1024 * 1024

F32 = jnp.float32
BF16 = jnp.bfloat16


def _dot(a, b):
    return jnp.dot(a, b, preferred_element_type=F32)


def _sigmoid(v):
    return 1.0 / (1.0 + jnp.exp2(v * (-LOG2_E)))


def _modulated_rms_norm(x, gain, scale, shift):
    ms = jnp.mean(x * x, axis=-1, keepdims=True)
    return (x * lax.rsqrt(ms + EPS)) * (gain * (1.0 + scale)) + shift


def _mod_kernel(*refs, n_cast):
    c_ref, w_ref, b_ref = refs[:3]
    cast_src = refs[3:3 + n_cast]
    o_ref = refs[3 + n_cast]
    cast_dst = refs[4 + n_cast:]
    c = c_ref[...]
    ca = c * _sigmoid(c)
    o_ref[0] = _dot(ca.astype(BF16), w_ref[0].astype(BF16)) + b_ref[0]
    for src_ref, dst_ref in zip(cast_src, cast_dst):
        dst_ref[...] = src_ref[...].astype(BF16)


def _cast_row_block(rows, n_steps):
    block = BF16_ROWS
    while rows % block or rows // block > n_steps:
        block += BF16_ROWS
    return block


def _cast_spec(w, layer, n_steps, step_of):
    rows, cols = w.shape[1:]
    block = _cast_row_block(rows, n_steps)
    return pl.BlockSpec(
        (1, block, cols),
        lambda *idx: (layer, jnp.minimum(step_of(*idx), rows // block - 1), 0))


def _cast_shapes(stacks):
    return [jax.ShapeDtypeStruct((1,) + w.shape[1:], BF16) for w in stacks]


def _mod_call(c_pad, ada_w, ada_b, cast_first):
    n_layers, d_model, n_out = ada_w.shape
    rows = c_pad.shape[0]
    nb = MOD_COL_BLOCK
    col_blocks = n_out // nb
    n_steps = n_layers * col_blocks

    def step_of(l, j):
        return l * col_blocks + j

    outs = pl.pallas_call(
        functools.partial(_mod_kernel, n_cast=len(cast_first)),
        grid=(n_layers, col_blocks),
        in_specs=[
            pl.BlockSpec((rows, d_model), lambda l, j: (0, 0)),
            pl.BlockSpec((1, d_model, nb), lambda l, j: (l, 0, j)),
            pl.BlockSpec((1, 1, nb), lambda l, j: (l, 0, j)),
        ] + [_cast_spec(w, 0, n_steps, step_of) for w in cast_first],
        out_specs=[pl.BlockSpec((1, rows, nb), lambda l, j: (l, 0, j))]
        + [_cast_spec(w, 0, n_steps, step_of) for w in cast_first],
        out_shape=[jax.ShapeDtypeStruct((n_layers, rows, n_out), F32)] + _cast_shapes(cast_first),
        compiler_params=pltpu.CompilerParams(
            dimension_semantics=("arbitrary", "arbitrary"),
            vmem_limit_bytes=VMEM_LIMIT_BYTES),
        name="mod",
    )(c_pad, ada_w, ada_b.reshape(n_layers, 1, n_out), *cast_first)
    return outs[0], outs[1:]


def _causal_conv31(glu, abuf, caw_ref, cab_ref, ts):
    abuf[pl.ds(A_HALO, ts), :] = glu
    rb = CONV_ROW_BLOCK
    first_off = A_HALO - (CONV_A_WIDTH - 1)
    blocks = []
    for r0 in range(0, ts, rb):
        acc = None
        for b in range(SUBLANES):
            taps = [(o - first_off, o - b) for o in range(first_off, A_HALO + 1)
                    if o % SUBLANES == b]
            span = max(off for _, off in taps)
            win = abuf[pl.ds(r0 + b, rb + span), :]
            part = None
            for k, off in taps:
                term = caw_ref[k:k + 1, :] * win[off:off + rb]
                part = term if part is None else part + term
            acc = part if acc is None else acc + part
        blocks.append(acc)
    conv = jnp.concatenate(blocks, axis=0) + cab_ref[...]
    abuf[pl.ds(0, A_HALO), :] = abuf[pl.ds(ts, A_HALO), :]
    return conv


def _group_norm_silu(conv, gng_ref, gnb_ref, gmat):
    inv = 1.0 / HEAD_DIM
    conv_hi = conv.astype(BF16)
    conv_lo = (conv - conv_hi.astype(F32)).astype(BF16)
    mu = (_dot(conv_hi, gmat) + _dot(conv_lo, gmat)) * inv
    d = conv - mu
    var = _dot((d * d).astype(BF16), gmat) * inv
    y = d * lax.rsqrt(var + EPS) * gng_ref[...] + gnb_ref[...]
    return y * _sigmoid(y)


def _pool_mixer(pb, bbuf, poolw_ref, pools_ref, ts, is_first):
    bbuf[pl.ds(B_HALO, ts), :] = pb
    half = W_GROUP // 2
    ext = bbuf[pl.ds(0, ts + B_HALO), :]
    s2 = ext + pltpu.roll(ext, 1, 0)
    s4 = s2 + pltpu.roll(s2, 2, 0)
    s4_hi = s4[:, half:W_GROUP]
    s8 = s4_hi + pltpu.roll(s4_hi, 4, 0)
    s16 = s8[B_HALO:] + s8[B_HALO - 8:B_HALO - 8 + ts]
    bbuf[pl.ds(0, B_HALO), :] = bbuf[pl.ds(ts, B_HALO), :]

    low = lax.broadcasted_iota(jnp.int32, (1, half), 1) < (half // 2)
    sum_a = jnp.where(low, s2[B_HALO:, 0:half], s4[B_HALO:, 0:half])
    sum_b = jnp.where(low, s8[B_HALO:], s16)
    sums = jnp.concatenate([sum_a, sum_b], axis=-1)

    lane_full = lax.broadcasted_iota(jnp.int32, (1, W_GROUP), 1)
    win = jnp.where(lane_full < 64, 2.0,
                    jnp.where(lane_full < 128, 4.0, jnp.where(lane_full < 192, 8.0, 16.0)))
    pos = lax.broadcasted_iota(jnp.int32, (B_HALO, W_GROUP), 0).astype(F32) + 1.0
    count = jnp.where(is_first, jnp.minimum(pos, win), win)
    mean = jnp.concatenate([sums[0:B_HALO] / count, sums[B_HALO:] * (1.0 / win)], axis=0)
    y = mean - pb
    return _dot(y.astype(BF16), poolw_ref[...]) * pools_ref[...]


def _spatial_gating(u, v, lng_ref, lnb_ref, sguw_ref, sgub_ref, ts):
    mu = jnp.mean(v, axis=-1, keepdims=True)
    d = v - mu
    var = jnp.mean(d * d, axis=-1, keepdims=True)
    vn = (d * lax.rsqrt(var + EPS) * lng_ref[...] + lnb_ref[...]).astype(BF16)

    row = lax.broadcasted_iota(jnp.int32, (CHUNK, N_HEADS * CHUNK), 0)
    col = lax.broadcasted_iota(jnp.int32, (CHUNK, N_HEADS * CHUNK), 1)
    wcat = jnp.where((col & (CHUNK - 1)) <= row, sguw_ref[...], 0.0).astype(BF16)

    head_of_lane = lax.broadcasted_iota(jnp.int32, (CHUNK, W_GROUP), 1) // HEAD_DIM
    zero = jnp.zeros((CHUNK, W_GROUP), BF16)
    outs = []
    for c0 in range(0, ts, CHUNK):
        vc = vn[c0:c0 + CHUNK]
        vstack = jnp.concatenate(
            [jnp.where(head_of_lane == h, vc, zero) for h in range(N_HEADS)], axis=0)
        outs.append(_dot(wcat, vstack) + sgub_ref[...])
    return u * jnp.concatenate(outs, axis=0)


def _short_conv(bg, cgh, dbuf, cdw_ref, ts):
    dbuf[pl.ds(D_HALO, ts), :] = cgh
    conv = cdw_ref[SHORT_CONV_WIDTH - 1:SHORT_CONV_WIDTH, :] * cgh
    for k in range(SHORT_CONV_WIDTH - 1):
        back = SHORT_CONV_WIDTH - 1 - k
        conv = conv + cdw_ref[k:k + 1, :] * dbuf[pl.ds(D_HALO - back, ts), :]
    dbuf[pl.ds(0, D_HALO), :] = dbuf[pl.ds(ts, D_HALO), :]
    return bg * conv


def _ffn_gated(h, wg_ref, wu_ref, fcw_ref, gcarry, mbuf, ts, col_blocks):
    for c0, c1 in col_blocks:
        cols = slice(c0, c1)
        g = _dot(h, wg_ref[0, :, cols])
        u = _dot(h, wu_ref[0, :, cols])
        ext = jnp.concatenate([gcarry[:, cols], g], axis=0)
        conv = fcw_ref[FFN_CONV_WIDTH - 1:FFN_CONV_WIDTH, cols] * g
        for k in range(FFN_CONV_WIDTH - 1):
            back = FFN_CONV_WIDTH - 1 - k
            conv = conv + fcw_ref[k:k + 1, cols] * pltpu.roll(ext, back, 0)[F_HALO:]
        gcarry[:, cols] = g[ts - F_HALO:]
        mbuf[:, cols] = (conv * _sigmoid(conv) * u).astype(BF16)


def _layer_kernel(*refs, ts, d_model, d_ff, tiles_per_seq, layer, final_norm, n_cast):
    (x_ref, modc_ref, modp_ref, caw_ref, cdw_ref, fcw_ref, gmat_ref, poolw_ref, sguw_ref, sgub_ref,
     fing_ref, w_in_ref, w_out_ref, wg_ref, wu_ref, wd_ref) = refs[:N_LAYER_INPUTS - N_LAYER_VECTORS]
    caw_ref, cdw_ref, fcw_ref = caw_ref.at[0], cdw_ref.at[0], fcw_ref.at[0]
    poolw_ref, sguw_ref, sgub_ref = poolw_ref.at[0], sguw_ref.at[0], sgub_ref.at[0]
    n1g_ref, n2g_ref, cab_ref, gng_ref, gnb_ref, pools_ref, lng_ref, lnb_ref = (
        v.at[pl.ds(layer, 1)] for v in refs[N_LAYER_INPUTS - N_LAYER_VECTORS:N_LAYER_INPUTS])
    cast_src = refs[N_LAYER_INPUTS:N_LAYER_INPUTS + n_cast]
    o_ref = refs[N_LAYER_INPUTS + n_cast]
    cast_dst = refs[N_LAYER_INPUTS + n_cast + 1:N_LAYER_INPUTS + 2 * n_cast + 1]
    abuf, bbuf, dbuf, ybuf, hbuf, xmid, gcarry, mbuf = refs[N_LAYER_INPUTS + 2 * n_cast + 1:]

    for src_ref, dst_ref in zip(cast_src, cast_dst):
        dst_ref[...] = src_ref[...].astype(BF16)

    step = pl.program_id(0)
    mix_first = lax.rem(step, tiles_per_seq) == 0
    ffn_first = (step == 0) | (lax.rem(step - 1, tiles_per_seq) == 0)

    @pl.when(step == 0)
    def _():
        hbuf[...] = jnp.zeros(hbuf.shape, BF16)
        xmid[...] = jnp.zeros(xmid.shape, F32)

    @pl.when(mix_first)
    def _():
        abuf[pl.ds(0, A_HALO), :] = jnp.zeros((A_HALO, W_GROUP), F32)
        bbuf[pl.ds(0, B_HALO), :] = jnp.zeros((B_HALO, W_GROUP), F32)
        dbuf[pl.ds(0, D_HALO), :] = jnp.zeros((D_HALO, W_GROUP), F32)

    @pl.when(ffn_first)
    def _():
        gcarry[...] = jnp.zeros(gcarry.shape, F32)

    modc = modc_ref[0]
    modp = modp_ref[0]

    def part(mod, i):
        return mod[:, i * d_model:(i + 1) * d_model]

    x_res = xmid[...]
    col_blocks = [(c0, min(c0 + FFN_COL_BLOCK, d_ff)) for c0 in range(0, d_ff, FFN_COL_BLOCK)]
    _ffn_gated(hbuf[...], wg_ref, wu_ref, fcw_ref, gcarry, mbuf, ts, col_blocks)

    x = x_ref[0]
    h1 = _modulated_rms_norm(x, n1g_ref[...], part(modc, 1), part(modc, 0)).astype(BF16)

    def proj(first, n):
        return _dot(h1, w_in_ref[0, :, first * W_GROUP:(first + n) * W_GROUP])

    def col(p, i):
        return p[:, i * W_GROUP:(i + 1) * W_GROUP]

    pa = proj(0, 2)
    conv = _causal_conv31(col(pa, 0) * _sigmoid(col(pa, 1)), abuf, caw_ref, cab_ref, ts)
    pb = proj(2, 1)
    pc = proj(3, 2)
    pd = proj(5, 3)

    f = _dot(mbuf[...], wd_ref[0])

    yb = _pool_mixer(pb, bbuf, poolw_ref, pools_ref, ts, mix_first)
    ybuf[:, 0:W_GROUP] = yb.astype(BF16)
    yc = _spatial_gating(col(pc, 0), col(pc, 1), lng_ref, lnb_ref, sguw_ref, sgub_ref, ts)
    ybuf[:, W_GROUP:2 * W_GROUP] = yc.astype(BF16)
    yd = _short_conv(col(pd, 0), col(pd, 1) * col(pd, 2), dbuf, cdw_ref, ts)
    ybuf[:, 2 * W_GROUP:3 * W_GROUP] = yd.astype(BF16)
    y = _dot(ybuf[...], w_out_ref[0, W_GROUP:4 * W_GROUP, :])
    ya = _group_norm_silu(conv, gng_ref, gnb_ref, gmat_ref[...])
    y = y + _dot(ya.astype(BF16), w_out_ref[0, 0:W_GROUP, :])
    x_mid = x + part(modc, 2) * y
    xmid[...] = x_mid
    hbuf[...] = _modulated_rms_norm(x_mid, n2g_ref[...], part(modc, 4), part(modc, 3)).astype(BF16)

    out = x_res + part(modp, 5) * f
    if final_norm:
        ms = jnp.mean(out * out, axis=-1, keepdims=True)
        out = out * lax.rsqrt(ms + EPS) * fing_ref[...]
    o_ref[0] = out


def _whole(shape):
    zeros = (0,) * len(shape)
    return pl.BlockSpec(shape, lambda *_: zeros, pipeline_mode=pl.Buffered(1))


def _layer(stacked, layer):
    index = (layer,) + (0,) * (stacked.ndim - 1)
    return pl.BlockSpec((1,) + stacked.shape[1:], lambda *_: index, pipeline_mode=pl.Buffered(1))


def _layer_call(x, mod_l, layer, final_norm, weights, vectors, caw, cdw, fcw, gmat, poolw, sguw, sgub,
                fing, cast_next):
    batch, seq, d_model = x.shape
    w_in, w_out, wg, wu, wd = weights
    d_ff = wg.shape[2]
    ts = SEQ_TILE
    tiles_per_seq = seq // ts
    n_tiles = batch * tiles_per_seq

    def cur(i):
        return jnp.minimum(i, n_tiles - 1)

    def prev(i):
        return jnp.maximum(i - 1, 0)

    def x_spec(tile):
        return pl.BlockSpec((1, ts, d_model),
                            lambda i: (tile(i) // tiles_per_seq, tile(i) % tiles_per_seq, 0))

    def mod_spec(tile):
        return pl.BlockSpec((1, 1, N_MOD * d_model), lambda i: (tile(i) // tiles_per_seq, 0, 0))

    def cast_spec(w, which):
        return _cast_spec(w, which, n_tiles, lambda i: i)

    per_layer = (caw, cdw, fcw)
    outs = pl.pallas_call(
        functools.partial(_layer_kernel, ts=ts, d_model=d_model, d_ff=d_ff,
                          tiles_per_seq=tiles_per_seq, layer=layer, final_norm=final_norm,
                          n_cast=len(cast_next)),
        grid=(n_tiles + 1,),
        in_specs=[x_spec(cur), mod_spec(cur), mod_spec(prev)]
        + [_layer(p, layer) for p in per_layer] + [_whole(gmat.shape)]
        + [_layer(p, layer) for p in (poolw, sguw, sgub)] + [_whole(fing.shape)]
        + [_layer(w, 0) for w in weights] + [_whole(v.shape) for v in vectors]
        + [cast_spec(w, layer + 1) for w in cast_next],
        out_specs=[x_spec(prev)] + [cast_spec(w, 0) for w in cast_next],
        out_shape=[jax.ShapeDtypeStruct(x.shape, x.dtype)] + _cast_shapes(cast_next),
        scratch_shapes=[
            pltpu.VMEM((ts + A_HALO, W_GROUP), F32),
            pltpu.VMEM((ts + B_HALO, W_GROUP), F32),
            pltpu.VMEM((ts + D_HALO, W_GROUP), F32),
            pltpu.VMEM((ts, 3 * W_GROUP), BF16),
            pltpu.VMEM((ts, d_model), BF16),
            pltpu.VMEM((ts, d_model), F32),
            pltpu.VMEM((F_HALO, d_ff), F32),
            pltpu.VMEM((ts, d_ff), BF16),
        ],
        compiler_params=pltpu.CompilerParams(
            dimension_semantics=("arbitrary",),
            vmem_limit_bytes=VMEM_LIMIT_BYTES),
        name="layer",
    )(x, mod_l, mod_l, *per_layer, gmat, poolw, sguw, sgub, fing, *weights, *vectors, *cast_next)
    return outs[0], outs[1:]


def _block_diag(blocks):
    layers, g, n, _ = blocks.shape
    eye = jnp.eye(g, dtype=blocks.dtype)
    return jnp.einsum("lgij,gh->lgihj", blocks, eye).reshape(layers, g * n, g * n)


def kernel(x, c, norm1_g, ada_w, ada_b, w_in, conv_a_w, conv_a_b, gn_a_g, gn_a_b, pool_w, pool_scale, sgu_ln_g, sgu_ln_b, sgu_w, sgu_b, conv_d_w, w_out, norm2_g, ffn_w_gate, ffn_w_up, ffn_conv_w, ffn_w_down, final_g):
    batch, seq, d_model = x.shape
    n_layers = ada_w.shape[0]
    assert seq % SEQ_TILE == 0 and SEQ_TILE % CHUNK == 0 and SEQ_TILE % CONV_ROW_BLOCK == 0
    assert ada_w.shape[2] % MOD_COL_BLOCK == 0

    stacks = (w_in, w_out, ffn_w_gate, ffn_w_up, ffn_w_down)
    c_pad = jnp.pad(c, ((0, -batch % SUBLANES), (0, 0)))
    mod, weights = _mod_call(c_pad, ada_w, ada_b, stacks)
    mod = mod[:, :batch].reshape(n_layers, batch, 1, N_MOD * d_model)

    vectors = (norm1_g, norm2_g, conv_a_b, gn_a_g, gn_a_b, pool_scale, sgu_ln_g, sgu_ln_b)
    head = jnp.arange(W_GROUP) // HEAD_DIM
    gmat = (head[:, None] == head[None, :]).astype(BF16)
    poolw = _block_diag(pool_w).astype(BF16)
    sguw = jnp.transpose(sgu_w, (0, 2, 1, 3)).reshape(n_layers, CHUNK, N_HEADS * CHUNK)
    sgub = jnp.repeat(jnp.transpose(sgu_b, (0, 2, 1)), HEAD_DIM, axis=2)
    fing = final_g.reshape(1, -1)
    for l in range(n_layers):
        last = l == n_layers - 1
        x, weights = _layer_call(
            x, mod[l], l, last, weights, vectors, conv_a_w, conv_d_w, ffn_conv_w, gmat, poolw, sguw,
            sgub, fing, cast_next=() if last else stacks)
    return x
```

```python
import functools

import jax
import jax.numpy as jnp
from jax import lax
from jax.experimental import pallas as pl
from jax.experimental.pallas import tpu as pltpu

EPS = 1e-6
LOG2_E = 1.4426950408889634
N_MOD = 6
W_GROUP = 256
HEAD_DIM = 64
N_HEADS = W_GROUP // HEAD_DIM
CHUNK = 128
CONV_A_WIDTH = 31
POOL_WINDOWS = (2, 4, 8, 16)
POOL_GROUP = W_GROUP // len(POOL_WINDOWS)
SHORT_CONV_WIDTH = 3
FFN_CONV_WIDTH = 3

SUBLANES = 8
A_HALO = 32
B_HALO = 16
D_HALO = 8
F_HALO = 8
CONV_ROW_BLOCK = 64

SEQ_TILE = 512
MOD_COL_BLOCK = 1536
FFN_COL_BLOCK = 512
N_LAYER_INPUTS = 24
N_LAYER_VECTORS = 8
BF16_ROWS = 16
VMEM_LIMIT_BYTES = 56 * 1024 * 1024

F32 = jnp.float32
BF16 = jnp.bfloat16


def _dot(a, b):
    return jnp.dot(a, b, preferred_element_type=F32)


def _sigmoid(v):
    return 1.0 / (1.0 + jnp.exp2(v * (-LOG2_E)))


def _modulated_rms_norm(x, gain, scale, shift):
    ms = jnp.mean(x * x, axis=-1, keepdims=True)
    return (x * lax.rsqrt(ms + EPS)) * (gain * (1.0 + scale)) + shift


def _mod_kernel(*refs, n_cast):
    c_ref, w_ref, b_ref = refs[:3]
    cast_src = refs[3:3 + n_cast]
    o_ref = refs[3 + n_cast]
    cast_dst = refs[4 + n_cast:]
    c = c_ref[...]
    ca = c * _sigmoid(c)
    o_ref[0] = _dot(ca.astype(BF16), w_ref[0].astype(BF16)) + b_ref[0]
    for src_ref, dst_ref in zip(cast_src, cast_dst):
        dst_ref[...] = src_ref[...].astype(BF16)


def _cast_row_block(rows, n_steps):
    block = BF16_ROWS
    while rows % block or rows // block > n_steps:
        block += BF16_ROWS
    return block


def _cast_spec(w, layer, n_steps, step_of):
    rows, cols = w.shape[1:]
    assert rows % BF16_ROWS == 0, (rows, BF16_ROWS)
    block = _cast_row_block(rows, n_steps)
    return pl.BlockSpec(
        (1, block, cols),
        lambda *idx: (layer, jnp.minimum(step_of(*idx), rows // block - 1), 0))


def _cast_shapes(stacks):
    return [jax.ShapeDtypeStruct((1,) + w.shape[1:], BF16) for w in stacks]


def _mod_call(c_pad, ada_w, ada_b, cast_first):
    n_layers, d_model, n_out = ada_w.shape
    rows = c_pad.shape[0]
    nb = MOD_COL_BLOCK
    col_blocks = n_out // nb
    n_steps = n_layers * col_blocks

    def step_of(l, j):
        return l * col_blocks + j

    outs = pl.pallas_call(
        functools.partial(_mod_kernel, n_cast=len(cast_first)),
        grid=(n_layers, col_blocks),
        in_specs=[
            pl.BlockSpec((rows, d_model), lambda l, j: (0, 0)),
            pl.BlockSpec((1, d_model, nb), lambda l, j: (l, 0, j)),
            pl.BlockSpec((1, 1, nb), lambda l, j: (l, 0, j)),
        ] + [_cast_spec(w, 0, n_steps, step_of) for w in cast_first],
        out_specs=[pl.BlockSpec((1, rows, nb), lambda l, j: (l, 0, j))]
        + [_cast_spec(w, 0, n_steps, step_of) for w in cast_first],
        out_shape=[jax.ShapeDtypeStruct((n_layers, rows, n_out), F32)] + _cast_shapes(cast_first),
        compiler_params=pltpu.CompilerParams(
            dimension_semantics=("arbitrary", "arbitrary"),
            vmem_limit_bytes=VMEM_LIMIT_BYTES),
        name="mod",
    )(c_pad, ada_w, ada_b.reshape(n_layers, 1, n_out), *cast_first)
    return outs[0], outs[1:]


def _causal_conv31(glu, abuf, caw_ref, cab_ref, ts):
    abuf[pl.ds(A_HALO, ts), :] = glu
    rb = CONV_ROW_BLOCK
    first_off = A_HALO - (CONV_A_WIDTH - 1)
    blocks = []
    for r0 in range(0, ts, rb):
        acc = None
        for b in range(SUBLANES):
            taps = [(o - first_off, o - b) for o in range(first_off, A_HALO + 1)
                    if o % SUBLANES == b]
            span = max(off for _, off in taps)
            win = abuf[pl.ds(r0 + b, rb + span), :]
            part = None
            for k, off in taps:
                term = caw_ref[k:k + 1, :] * win[off:off + rb]
                part = term if part is None else part + term
            acc = part if acc is None else acc + part
        blocks.append(acc)
    conv = jnp.concatenate(blocks, axis=0) + cab_ref[...]
    abuf[pl.ds(0, A_HALO), :] = abuf[pl.ds(ts, A_HALO), :]
    return conv


def _group_norm_silu(conv, gng_ref, gnb_ref, gmat):
    inv = 1.0 / HEAD_DIM
    conv_hi = conv.astype(BF16)
    conv_lo = (conv - conv_hi.astype(F32)).astype(BF16)
    mu = (_dot(conv_hi, gmat) + _dot(conv_lo, gmat)) * inv
    d = conv - mu
    var = _dot((d * d).astype(BF16), gmat) * inv
    y = d * lax.rsqrt(var + EPS) * gng_ref[...] + gnb_ref[...]
    return y * _sigmoid(y)


def _pool_mixer(pb, bbuf, poolw_ref, pools_ref, ts, is_first):
    bbuf[pl.ds(B_HALO, ts), :] = pb
    half = W_GROUP // 2
    ext = bbuf[pl.ds(0, ts + B_HALO), :]
    s2 = ext + pltpu.roll(ext, 1, 0)
    s4 = s2 + pltpu.roll(s2, 2, 0)
    s4_hi = s4[:, half:W_GROUP]
    s8 = s4_hi + pltpu.roll(s4_hi, 4, 0)
    s16 = s8[B_HALO:] + s8[B_HALO - 8:B_HALO - 8 + ts]
    bbuf[pl.ds(0, B_HALO), :] = bbuf[pl.ds(ts, B_HALO), :]

    low = lax.broadcasted_iota(jnp.int32, (1, half), 1) < (half // 2)
    sum_a = jnp.where(low, s2[B_HALO:, 0:half], s4[B_HALO:, 0:half])
    sum_b = jnp.where(low, s8[B_HALO:], s16)
    sums = jnp.concatenate([sum_a, sum_b], axis=-1)

    group = lax.broadcasted_iota(jnp.int32, (1, W_GROUP), 1) // POOL_GROUP
    win = jnp.full((1, W_GROUP), float(POOL_WINDOWS[-1]), F32)
    for g in reversed(range(len(POOL_WINDOWS) - 1)):
        win = jnp.where(group == g, float(POOL_WINDOWS[g]), win)
    pos = lax.broadcasted_iota(jnp.int32, (B_HALO, W_GROUP), 0).astype(F32) + 1.0
    count = jnp.where(is_first, jnp.minimum(pos, win), win)
    mean = jnp.concatenate([sums[0:B_HALO] / count, sums[B_HALO:] * (1.0 / win)], axis=0)
    y = mean - pb
    return _dot(y.astype(BF16), poolw_ref[...]) * pools_ref[...]


def _spatial_gating(u, v, lng_ref, lnb_ref, sguw_ref, sgub_ref, ts):
    mu = jnp.mean(v, axis=-1, keepdims=True)
    d = v - mu
    var = jnp.mean(d * d, axis=-1, keepdims=True)
    vn = (d * lax.rsqrt(var + EPS) * lng_ref[...] + lnb_ref[...]).astype(BF16)

    row = lax.broadcasted_iota(jnp.int32, (CHUNK, N_HEADS * CHUNK), 0)
    col = lax.broadcasted_iota(jnp.int32, (CHUNK, N_HEADS * CHUNK), 1)
    wcat = jnp.where((col & (CHUNK - 1)) <= row, sguw_ref[...], 0.0).astype(BF16)

    head_of_lane = lax.broadcasted_iota(jnp.int32, (CHUNK, W_GROUP), 1) // HEAD_DIM
    zero = jnp.zeros((CHUNK, W_GROUP), BF16)
    outs = []
    for c0 in range(0, ts, CHUNK):
        vc = vn[c0:c0 + CHUNK]
        vstack = jnp.concatenate(
            [jnp.where(head_of_lane == h, vc, zero) for h in range(N_HEADS)], axis=0)
        outs.append(_dot(wcat, vstack) + sgub_ref[...])
    return u * jnp.concatenate(outs, axis=0)


def _short_conv(bg, cgh, dbuf, cdw_ref, ts):
    dbuf[pl.ds(D_HALO, ts), :] = cgh
    conv = cdw_ref[SHORT_CONV_WIDTH - 1:SHORT_CONV_WIDTH, :] * cgh
    for k in range(SHORT_CONV_WIDTH - 1):
        back = SHORT_CONV_WIDTH - 1 - k
        conv = conv + cdw_ref[k:k + 1, :] * dbuf[pl.ds(D_HALO - back, ts), :]
    dbuf[pl.ds(0, D_HALO), :] = dbuf[pl.ds(ts, D_HALO), :]
    return bg * conv


def _ffn_gated(h, wg_ref, wu_ref, fcw_ref, gcarry, mbuf, ts, col_blocks):
    for c0, c1 in col_blocks:
        cols = slice(c0, c1)
        g = _dot(h, wg_ref[0, :, cols])
        u = _dot(h, wu_ref[0, :, cols])
        ext = jnp.concatenate([gcarry[:, cols], g], axis=0)
        conv = fcw_ref[FFN_CONV_WIDTH - 1:FFN_CONV_WIDTH, cols] * g
        for k in range(FFN_CONV_WIDTH - 1):
            back = FFN_CONV_WIDTH - 1 - k
            conv = conv + fcw_ref[k:k + 1, cols] * pltpu.roll(ext, back, 0)[F_HALO:]
        gcarry[:, cols] = g[ts - F_HALO:]
        mbuf[:, cols] = (conv * _sigmoid(conv) * u).astype(BF16)


def _layer_kernel(*refs, ts, d_model, d_ff, tiles_per_seq, layer, final_norm, n_cast):
    (x_ref, modc_ref, modp_ref, caw_ref, cdw_ref, fcw_ref, gmat_ref, poolw_ref, sguw_ref, sgub_ref,
     fing_ref, w_in_ref, w_out_ref, wg_ref, wu_ref, wd_ref) = refs[:N_LAYER_INPUTS - N_LAYER_VECTORS]
    caw_ref, cdw_ref, fcw_ref = caw_ref.at[0], cdw_ref.at[0], fcw_ref.at[0]
    poolw_ref, sguw_ref, sgub_ref = poolw_ref.at[0], sguw_ref.at[0], sgub_ref.at[0]
    n1g_ref, n2g_ref, cab_ref, gng_ref, gnb_ref, pools_ref, lng_ref, lnb_ref = (
        v.at[pl.ds(layer, 1)] for v in refs[N_LAYER_INPUTS - N_LAYER_VECTORS:N_LAYER_INPUTS])
    cast_src = refs[N_LAYER_INPUTS:N_LAYER_INPUTS + n_cast]
    o_ref = refs[N_LAYER_INPUTS + n_cast]
    cast_dst = refs[N_LAYER_INPUTS + n_cast + 1:N_LAYER_INPUTS + 2 * n_cast + 1]
    abuf, bbuf, dbuf, ybuf, hbuf, xmid, gcarry, mbuf = refs[N_LAYER_INPUTS + 2 * n_cast + 1:]

    for src_ref, dst_ref in zip(cast_src, cast_dst):
        dst_ref[...] = src_ref[...].astype(BF16)

    step = pl.program_id(0)
    mix_first = lax.rem(step, tiles_per_seq) == 0
    ffn_first = (step == 0) | (lax.rem(step - 1, tiles_per_seq) == 0)

    @pl.when(step == 0)
    def _():
        hbuf[...] = jnp.zeros(hbuf.shape, BF16)
        xmid[...] = jnp.zeros(xmid.shape, F32)

    @pl.when(mix_first)
    def _():
        abuf[pl.ds(0, A_HALO), :] = jnp.zeros((A_HALO, W_GROUP), F32)
        bbuf[pl.ds(0, B_HALO), :] = jnp.zeros((B_HALO, W_GROUP), F32)
        dbuf[pl.ds(0, D_HALO), :] = jnp.zeros((D_HALO, W_GROUP), F32)

    @pl.when(ffn_first)
    def _():
        gcarry[...] = jnp.zeros(gcarry.shape, F32)

    modc = modc_ref[0]
    modp = modp_ref[0]

    def part(mod, i):
        return mod[:, i * d_model:(i + 1) * d_model]

    x_res = xmid[...]
    col_blocks = [(c0, min(c0 + FFN_COL_BLOCK, d_ff)) for c0 in range(0, d_ff, FFN_COL_BLOCK)]
    _ffn_gated(hbuf[...], wg_ref, wu_ref, fcw_ref, gcarry, mbuf, ts, col_blocks)

    x = x_ref[0]
    h1 = _modulated_rms_norm(x, n1g_ref[...], part(modc, 1), part(modc, 0)).astype(BF16)

    def proj(first, n):
        return _dot(h1, w_in_ref[0, :, first * W_GROUP:(first + n) * W_GROUP])

    def col(p, i):
        return p[:, i * W_GROUP:(i + 1) * W_GROUP]

    pa = proj(0, 2)
    conv = _causal_conv31(col(pa, 0) * _sigmoid(col(pa, 1)), abuf, caw_ref, cab_ref, ts)
    pb = proj(2, 1)
    pc = proj(3, 2)
    pd = proj(5, 3)

    f = _dot(mbuf[...], wd_ref[0])

    yb = _pool_mixer(pb, bbuf, poolw_ref, pools_ref, ts, mix_first)
    ybuf[:, 0:W_GROUP] = yb.astype(BF16)
    yc = _spatial_gating(col(pc, 0), col(pc, 1), lng_ref, lnb_ref, sguw_ref, sgub_ref, ts)
    ybuf[:, W_GROUP:2 * W_GROUP] = yc.astype(BF16)
    yd = _short_conv(col(pd, 0), col(pd, 1) * col(pd, 2), dbuf, cdw_ref, ts)
    ybuf[:, 2 * W_GROUP:3 * W_GROUP] = yd.astype(BF16)
    y = _dot(ybuf[...], w_out_ref[0, W_GROUP:4 * W_GROUP, :])
    ya = _group_norm_silu(conv, gng_ref, gnb_ref, gmat_ref[...])
    y = y + _dot(ya.astype(BF16), w_out_ref[0, 0:W_GROUP, :])
    x_mid = x + part(modc, 2) * y
    xmid[...] = x_mid
    hbuf[...] = _modulated_rms_norm(x_mid, n2g_ref[...], part(modc, 4), part(modc, 3)).astype(BF16)

    out = x_res + part(modp, 5) * f
    if final_norm:
        ms = jnp.mean(out * out, axis=-1, keepdims=True)
        out = out * lax.rsqrt(ms + EPS) * fing_ref[...]
    o_ref[0] = out


def _whole(shape):
    zeros = (0,) * len(shape)
    return pl.BlockSpec(shape, lambda *_: zeros, pipeline_mode=pl.Buffered(1))


def _layer(stacked, layer):
    index = (layer,) + (0,) * (stacked.ndim - 1)
    return pl.BlockSpec((1,) + stacked.shape[1:], lambda *_: index, pipeline_mode=pl.Buffered(1))


def _layer_call(x, mod_l, layer, final_norm, weights, vectors, caw, cdw, fcw, gmat, poolw, sguw, sgub,
                fing, cast_next):
    batch, seq, d_model = x.shape
    w_in, w_out, wg, wu, wd = weights
    d_ff = wg.shape[2]
    ts = SEQ_TILE
    tiles_per_seq = seq // ts
    n_tiles = batch * tiles_per_seq

    def cur(i):
        return jnp.minimum(i, n_tiles - 1)

    def prev(i):
        return jnp.maximum(i - 1, 0)

    def x_spec(tile):
        return pl.BlockSpec((1, ts, d_model),
                            lambda i: (tile(i) // tiles_per_seq, tile(i) % tiles_per_seq, 0))

    def mod_spec(tile):
        return pl.BlockSpec((1, 1, N_MOD * d_model), lambda i: (tile(i) // tiles_per_seq, 0, 0))

    def cast_spec(w, which):
        return _cast_spec(w, which, n_tiles, lambda i: i)

    per_layer = (caw, cdw, fcw)
    outs = pl.pallas_call(
        functools.partial(_layer_kernel, ts=ts, d_model=d_model, d_ff=d_ff,
                          tiles_per_seq=tiles_per_seq, layer=layer, final_norm=final_norm,
                          n_cast=len(cast_next)),
        grid=(n_tiles + 1,),
        in_specs=[x_spec(cur), mod_spec(cur), mod_spec(prev)]
        + [_layer(p, layer) for p in per_layer] + [_whole(gmat.shape)]
        + [_layer(p, layer) for p in (poolw, sguw, sgub)] + [_whole(fing.shape)]
        + [_layer(w, 0) for w in weights] + [_whole(v.shape) for v in vectors]
        + [cast_spec(w, layer + 1) for w in cast_next],
        out_specs=[x_spec(prev)] + [cast_spec(w, 0) for w in cast_next],
        out_shape=[jax.ShapeDtypeStruct(x.shape, x.dtype)] + _cast_shapes(cast_next),
        scratch_shapes=[
            pltpu.VMEM((ts + A_HALO, W_GROUP), F32),
            pltpu.VMEM((ts + B_HALO, W_GROUP), F32),
            pltpu.VMEM((ts + D_HALO, W_GROUP), F32),
            pltpu.VMEM((ts, 3 * W_GROUP), BF16),
            pltpu.VMEM((ts, d_model), BF16),
            pltpu.VMEM((ts, d_model), F32),
            pltpu.VMEM((F_HALO, d_ff), F32),
            pltpu.VMEM((ts, d_ff), BF16),
        ],
        compiler_params=pltpu.CompilerParams(
            dimension_semantics=("arbitrary",),
            vmem_limit_bytes=VMEM_LIMIT_BYTES),
        name="layer",
    )(x, mod_l, mod_l, *per_layer, gmat, poolw, sguw, sgub, fing, *weights, *vectors, *cast_next)
    return outs[0], outs[1:]


def _block_diag(blocks):
    layers, g, n, _ = blocks.shape
    eye = jnp.eye(g, dtype=blocks.dtype)
    return jnp.einsum("lgij,gh->lgihj", blocks, eye).reshape(layers, g * n, g * n)


def kernel(x, c, norm1_g, ada_w, ada_b, w_in, conv_a_w, conv_a_b, gn_a_g, gn_a_b, pool_w, pool_scale, sgu_ln_g, sgu_ln_b, sgu_w, sgu_b, conv_d_w, w_out, norm2_g, ffn_w_gate, ffn_w_up, ffn_conv_w, ffn_w_down, final_g):
    batch, seq, d_model = x.shape
    n_layers = ada_w.shape[0]
    assert seq % SEQ_TILE == 0 and SEQ_TILE % CHUNK == 0 and SEQ_TILE % CONV_ROW_BLOCK == 0
    assert ada_w.shape[2] % MOD_COL_BLOCK == 0

    stacks = (w_in, w_out, ffn_w_gate, ffn_w_up, ffn_w_down)
    c_pad = jnp.pad(c, ((0, -batch % SUBLANES), (0, 0)))
    mod, weights = _mod_call(c_pad, ada_w, ada_b, stacks)
    mod = mod[:, :batch].reshape(n_layers, batch, 1, N_MOD * d_model)

    vectors = (norm1_g, norm2_g, conv_a_b, gn_a_g, gn_a_b, pool_scale, sgu_ln_g, sgu_ln_b)
    head = jnp.arange(W_GROUP) // HEAD_DIM
    gmat = (head[:, None] == head[None, :]).astype(BF16)
    poolw = _block_diag(pool_w).astype(BF16)
    sguw = jnp.transpose(sgu_w, (0, 2, 1, 3)).reshape(n_layers, CHUNK, N_HEADS * CHUNK)
    sgub = jnp.repeat(jnp.transpose(sgu_b, (0, 2, 1)), HEAD_DIM, axis=2)
    fing = final_g.reshape(1, -1)
    for l in range(n_layers):
        last = l == n_layers - 1
        x, weights = _layer_call(
            x, mod[l], l, last, weights, vectors, conv_a_w, conv_d_w, ffn_conv_w, gmat, poolw, sguw,
            sgub, fing, cast_next=() if last else stacks)
    return x
```
